```python
import jax, jax.numpy as jnp
from jax import lax
import numpy as np

D_MODEL = 2048
BATCH = 4
SEQ = 2048
DEPTH = 4
DEC_BATCH = 128
DEC_SEQ = 4
PAST_LEN = 16384
PAGE_SIZE = 128

N_MIXERS = 2
N_SSD_LAYERS = (DEPTH + 1) // 2
N_POOL_LAYERS = DEPTH // 2

SSD_EXPAND = 2
D_INNER = SSD_EXPAND * D_MODEL
SSD_HEAD_DIM = 64
SSD_HEADS = D_INNER // SSD_HEAD_DIM
SSD_GROUPS = 8
SSD_HPG = SSD_HEADS // SSD_GROUPS
D_STATE = 128
SSD_CONV = 4
SSD_CONV_DIM = D_INNER + 2 * SSD_GROUPS * D_STATE
SSD_IN_DIM = D_INNER + SSD_CONV_DIM + SSD_HEADS
SSD_CHUNK = 128

POOL_WINDOWS = (2, 4, 8, 16)
POOL_GROUPS = 4
D_POOL = D_MODEL
POOL_GROUP_DIM = D_POOL // POOL_GROUPS
POOL_BUF = max(POOL_WINDOWS) - 1

N_MEM = 256
XA_HEADS = 4
XA_HEAD_DIM = D_MODEL // XA_HEADS

D_FF = 5632
FFN_CONV = 3

EPS = 1e-6

kernel_name = "hybrid_ssd_pool_memxattn_convffn_step"


def rmsnorm(x, g):
    xf = x.astype(jnp.float32)
    y = xf * lax.rsqrt(jnp.mean(xf * xf, axis=-1, keepdims=True) + EPS)
    return (y * g.astype(jnp.float32)).astype(x.dtype)


def causal_dwconv(x_ext, w, b):
    width = w.shape[0]
    n = x_ext.shape[1] - (width - 1)
    acc = x_ext[:, 0:n] * w[0]
    for k in range(1, width):
        acc = acc + x_ext[:, k:k + n] * w[k]
    return acc + b


def ssd_scan(x, dt, a, bm, cm, h0):
    bsz, n = x.shape[:2]
    q = min(SSD_CHUNK, n)
    nc = n // q
    xg = x.reshape(bsz, nc, q, SSD_GROUPS, SSD_HPG, SSD_HEAD_DIM)
    dtg = dt.reshape(bsz, nc, q, SSD_GROUPS, SSD_HPG)
    bg = bm.reshape(bsz, nc, q, SSD_GROUPS, D_STATE)
    cg = cm.reshape(bsz, nc, q, SSD_GROUPS, D_STATE)
    da = dtg * a.reshape(SSD_GROUPS, SSD_HPG)
    acum = jnp.cumsum(da, axis=2)
    xdt = xg * dtg[..., None]
    diff = acum[:, :, :, None] - acum[:, :, None, :]
    causal = jnp.tril(jnp.ones((q, q), dtype=bool))[:, :, None, None]
    decay = jnp.exp(jnp.where(causal, diff, -jnp.inf))
    cb = jnp.einsum('bcign,bcjgn->bcijg', cg, bg)
    y_diag = jnp.einsum('bcijgr,bcjgrp->bcigrp', cb[..., None] * decay, xdt)
    decay_last = jnp.exp(acum[:, :, -1:] - acum)
    chunk_states = jnp.einsum('bcjgn,bcjgr,bcjgrp->bcgrpn', bg, decay_last, xdt)
    chunk_decay = jnp.exp(acum[:, :, -1])

    def step(h, inp):
        s, d = inp
        return h * d[..., None, None] + s, h

    h0g = h0.reshape(bsz, SSD_GROUPS, SSD_HPG, SSD_HEAD_DIM, D_STATE)
    h_final, h_in = lax.scan(step, h0g, (jnp.moveaxis(chunk_states, 1, 0), jnp.moveaxis(chunk_decay, 1, 0)))
    h_in = jnp.moveaxis(h_in, 0, 1)
    y_off = jnp.einsum('bcign,bcgrpn,bcigr->bcigrp', cg, h_in, jnp.exp(acum))
    y = (y_diag + y_off).reshape(bsz, n, SSD_HEADS, SSD_HEAD_DIM)
    return y, h_final.reshape(bsz, SSD_HEADS, SSD_HEAD_DIM, D_STATE)


def ssd_mixer(u, conv_buf, h0, w_in, conv_w, conv_b, dt_bias, a_log, d_skip, g_norm, w_out):
    bsz, n, _ = u.shape
    proj = u @ w_in
    z = proj[..., :D_INNER]
    xbc = proj[..., D_INNER:D_INNER + SSD_CONV_DIM]
    dt_raw = proj[..., D_INNER + SSD_CONV_DIM:]
    xbc_ext = jnp.concatenate([conv_buf.astype(xbc.dtype), xbc], axis=1)
    new_buf = xbc_ext[:, -(SSD_CONV - 1):]
    xbc = jax.nn.silu(causal_dwconv(xbc_ext, conv_w, conv_b)).astype(jnp.float32)
    xs = xbc[..., :D_INNER].reshape(bsz, n, SSD_HEADS, SSD_HEAD_DIM)
    bm = xbc[..., D_INNER:D_INNER + SSD_GROUPS * D_STATE].reshape(bsz, n, SSD_GROUPS, D_STATE)
    cm = xbc[..., D_INNER + SSD_GROUPS * D_STATE:].reshape(bsz, n, SSD_GROUPS, D_STATE)
    dt = jax.nn.softplus(dt_raw.astype(jnp.float32) + dt_bias.astype(jnp.float32))
    a = -jnp.exp(a_log.astype(jnp.float32))
    y, h = ssd_scan(xs, dt, a, bm, cm, h0.astype(jnp.float32))
    y = y + xs * d_skip.astype(jnp.float32)[:, None]
    y = y.reshape(bsz, n, D_INNER) * jax.nn.silu(z.astype(jnp.float32))
    gdim = D_INNER // SSD_GROUPS
    y = rmsnorm(y.reshape(bsz, n, SSD_GROUPS, gdim), g_norm.reshape(SSD_GROUPS, gdim))
    out = y.reshape(bsz, n, D_INNER).astype(u.dtype) @ w_out
    return out, new_buf, h.astype(h0.dtype)


def pool_mixer(u, buf, pos0, w_in, w_grp, scale, w_out):
    bsz, n, _ = u.shape
    v = u @ w_in
    v_ext = jnp.concatenate([buf.astype(v.dtype), v], axis=1)
    new_buf = v_ext[:, -POOL_BUF:]
    cs = jnp.cumsum(v_ext.astype(jnp.float32), axis=1)
    cs = jnp.concatenate([jnp.zeros((bsz, 1, D_POOL), jnp.float32), cs], axis=1)
    end = cs[:, POOL_BUF + 1:POOL_BUF + 1 + n]
    pos = pos0 + jnp.arange(n, dtype=jnp.int32)
    means = []
    for gi, win in enumerate(POOL_WINDOWS):
        sl = slice(gi * POOL_GROUP_DIM, (gi + 1) * POOL_GROUP_DIM)
        start = cs[:, POOL_BUF + 1 - win:POOL_BUF + 1 - win + n, sl]
        cnt = jnp.minimum(pos + 1, win).astype(jnp.float32)
        means.append((end[..., sl] - start) / cnt[None, :, None])
    m = jnp.stack(means, axis=2)
    p = (m - v.astype(jnp.float32).reshape(bsz, n, POOL_GROUPS, POOL_GROUP_DIM)).astype(u.dtype)
    mixed = jnp.einsum('blgc,gcd->blgd', p, w_grp).reshape(bsz, n, D_POOL) * scale
    return mixed @ w_out, new_buf


def memory_kv(mem, g, w_k, w_v):
    bsz = mem.shape[0]
    m = rmsnorm(mem, g)
    k = (m @ w_k).reshape(bsz, N_MEM, XA_HEADS, XA_HEAD_DIM)
    v = (m @ w_v).reshape(bsz, N_MEM, XA_HEADS, XA_HEAD_DIM)
    return k, v


def cross_attn(u, k, v, w_q, w_o):
    bsz, n, _ = u.shape
    q = (u @ w_q).reshape(bsz, n, XA_HEADS, XA_HEAD_DIM)
    s = jnp.einsum('blhd,bmhd->bhlm', q, k.astype(q.dtype)).astype(jnp.float32) * (XA_HEAD_DIM ** -0.5)
    p = jax.nn.softmax(s, axis=-1).astype(u.dtype)
    o = jnp.einsum('bhlm,bmhd->blhd', p, v.astype(u.dtype)).reshape(bsz, n, D_MODEL)
    return o @ w_o


def conv_ffn(u, buf, w_gate, conv_w, conv_b, w_up, w_down):
    g = u @ w_gate
    g_ext = jnp.concatenate([buf.astype(g.dtype), g], axis=1)
    new_buf = g_ext[:, -(FFN_CONV - 1):]
    g = jax.nn.silu(causal_dwconv(g_ext, conv_w, conv_b))
    return (g * (u @ w_up)) @ w_down, new_buf


def run_trunk(x, pos0, mem_k, mem_v, ssm0, sconv0, pool0, fconv0, w):
    ssm_out, sconv_out, pool_out, fconv_out = [], [], [], []
    for i in range(DEPTH):
        j = i // N_MIXERS
        u = rmsnorm(x, w['norm_mix'][i])
        if i % N_MIXERS == 0:
            out, cbuf, h = ssd_mixer(u, sconv0[j], ssm0[j], w['ssd_w_in'][j], w['ssd_conv_w'][j], w['ssd_conv_b'][j],
                                     w['ssd_dt_bias'][j], w['ssd_a_log'][j], w['ssd_d'][j], w['ssd_norm'][j],
                                     w['ssd_w_out'][j])
            ssm_out.append(h)
            sconv_out.append(cbuf)
        else:
            out, pbuf = pool_mixer(u, pool0[j], pos0, w['pool_w_in'][j], w['pool_w_grp'][j], w['pool_scale'][j],
                                   w['pool_w_out'][j])
            pool_out.append(pbuf)
        x = x + out
        x = x + cross_attn(rmsnorm(x, w['norm_xattn'][i]), mem_k[i], mem_v[i], w['xa_w_q'][i], w['xa_w_o'][i])
        f, fbuf = conv_ffn(rmsnorm(x, w['norm_ffn'][i]), fconv0[i], w['ffn_w_gate'][i], w['ffn_conv_w'][i],
                           w['ffn_conv_b'][i], w['ffn_w_up'][i], w['ffn_w_down'][i])
        fconv_out.append(fbuf)
        x = x + f
    y = rmsnorm(x, w['norm_final'])
    return y, jnp.stack(ssm_out), jnp.stack(sconv_out), jnp.stack(pool_out), jnp.stack(fconv_out)


def setup_inputs(seed: int = 0) -> dict:
    key = jax.random.key(seed)
    keys = jax.random.split(key, 48)
    counter = iter(range(48))

    def nk():
        return keys[next(counter)]

    def nrm(shape, scale):
        return jax.random.normal(nk(), shape, jnp.float32) * scale

    def gain(shape):
        return 1.0 + 0.05 * jax.random.normal(nk(), shape, jnp.float32)

    d = {}
    d['x_prompt'] = nrm((BATCH, SEQ, D_MODEL), 1.0)
    d['x_sample'] = nrm((DEC_BATCH, DEC_SEQ, D_MODEL), 1.0)
    d['mem_prompt'] = nrm((BATCH, N_MEM, D_MODEL), 1.0)
    d['cache_mem_k'] = nrm((DEPTH, DEC_BATCH, N_MEM, XA_HEADS, XA_HEAD_DIM), 1.0)
    d['cache_mem_v'] = nrm((DEPTH, DEC_BATCH, N_MEM, XA_HEADS, XA_HEAD_DIM), 1.0)
    d['state_ssm'] = nrm((N_SSD_LAYERS, DEC_BATCH, SSD_HEADS, SSD_HEAD_DIM, D_STATE), 0.1)
    d['state_ssm_conv'] = nrm((N_SSD_LAYERS, DEC_BATCH, SSD_CONV - 1, SSD_CONV_DIM), 1.0)
    d['state_pool'] = nrm((N_POOL_LAYERS, DEC_BATCH, POOL_BUF, D_POOL), 1.0)
    d['state_ffn_conv'] = nrm((DEPTH, DEC_BATCH, FFN_CONV - 1, D_FF), 1.0)
    d['norm_mix'] = gain((DEPTH, D_MODEL))
    d['norm_xattn'] = gain((DEPTH, D_MODEL))
    d['norm_ffn'] = gain((DEPTH, D_MODEL))
    d['norm_mem'] = gain((DEPTH, D_MODEL))
    d['norm_final'] = gain((D_MODEL,))
    d['ssd_w_in'] = nrm((N_SSD_LAYERS, D_MODEL, SSD_IN_DIM), D_MODEL ** -0.5)
    d['ssd_conv_w'] = nrm((N_SSD_LAYERS, SSD_CONV, SSD_CONV_DIM), SSD_CONV ** -0.5)
    d['ssd_conv_b'] = nrm((N_SSD_LAYERS, SSD_CONV_DIM), 0.01)
    dt0 = jnp.exp(jax.random.uniform(nk(), (N_SSD_LAYERS, SSD_HEADS), jnp.float32, np.log(1e-3), np.log(1e-1)))
    d['ssd_dt_bias'] = dt0 + jnp.log(-jnp.expm1(-dt0))
    d['ssd_a_log'] = jnp.log(jax.random.uniform(nk(), (N_SSD_LAYERS, SSD_HEADS), jnp.float32, 1.0, 16.0))
    d['ssd_d'] = gain((N_SSD_LAYERS, SSD_HEADS))
    d['ssd_norm'] = gain((N_SSD_LAYERS, D_INNER))
    d['ssd_w_out'] = nrm((N_SSD_LAYERS, D_INNER, D_MODEL), D_INNER ** -0.5)
    d['pool_w_in'] = nrm((N_POOL_LAYERS, D_MODEL, D_POOL), D_MODEL ** -0.5)
    d['pool_w_grp'] = nrm((N_POOL_LAYERS, POOL_GROUPS, POOL_GROUP_DIM, POOL_GROUP_DIM), POOL_GROUP_DIM ** -0.5)
    d['pool_scale'] = gain((N_POOL_LAYERS, D_POOL))
    d['pool_w_out'] = nrm((N_POOL_LAYERS, D_POOL, D_MODEL), D_POOL ** -0.5)
    d['xa_w_q'] = nrm((DEPTH, D_MODEL, D_MODEL), D_MODEL ** -0.5)
    d['xa_w_k'] = nrm((DEPTH, D_MODEL, D_MODEL), D_MODEL ** -0.5)
    d['xa_w_v'] = nrm((DEPTH, D_MODEL, D_MODEL), D_MODEL ** -0.5)
    d['xa_w_o'] = nrm((DEPTH, D_MODEL, D_MODEL), D_MODEL ** -0.5)
    d['ffn_w_gate'] = nrm((DEPTH, D_MODEL, D_FF), D_MODEL ** -0.5)
    d['ffn_conv_w'] = nrm((DEPTH, FFN_CONV, D_FF), FFN_CONV ** -0.5)
    d['ffn_conv_b'] = nrm((DEPTH, D_FF), 0.01)
    d['ffn_w_up'] = nrm((DEPTH, D_MODEL, D_FF), D_MODEL ** -0.5)
    d['ffn_w_down'] = nrm((DEPTH, D_FF, D_MODEL), D_FF ** -0.5)
    return d


def reference(x_prompt, x_sample, mem_prompt, cache_mem_k, cache_mem_v, state_ssm, state_ssm_conv, state_pool,
              state_ffn_conv, norm_mix, norm_xattn, norm_ffn, norm_mem, norm_final,
              ssd_w_in, ssd_conv_w, ssd_conv_b, ssd_dt_bias, ssd_a_log, ssd_d, ssd_norm, ssd_w_out,
              pool_w_in, pool_w_grp, pool_scale, pool_w_out,
              xa_w_q, xa_w_k, xa_w_v, xa_w_o,
              ffn_w_gate, ffn_conv_w, ffn_conv_b, ffn_w_up, ffn_w_down):
    w = dict(norm_mix=norm_mix, norm_xattn=norm_xattn, norm_ffn=norm_ffn, norm_final=norm_final,
             ssd_w_in=ssd_w_in, ssd_conv_w=ssd_conv_w, ssd_conv_b=ssd_conv_b, ssd_dt_bias=ssd_dt_bias,
             ssd_a_log=ssd_a_log, ssd_d=ssd_d, ssd_norm=ssd_norm, ssd_w_out=ssd_w_out,
             pool_w_in=pool_w_in, pool_w_grp=pool_w_grp, pool_scale=pool_scale, pool_w_out=pool_w_out,
             xa_w_q=xa_w_q, xa_w_o=xa_w_o,
             ffn_w_gate=ffn_w_gate, ffn_conv_w=ffn_conv_w, ffn_conv_b=ffn_conv_b, ffn_w_up=ffn_w_up,
             ffn_w_down=ffn_w_down)
    mk, mv = [], []
    for i in range(DEPTH):
        k, v = memory_kv(mem_prompt, norm_mem[i], xa_w_k[i], xa_w_v[i])
        mk.append(k)
        mv.append(v)
    new_mem_k_prompt = jnp.stack(mk)
    new_mem_v_prompt = jnp.stack(mv)
    fdt = x_prompt.dtype
    ssm0_p = jnp.zeros((N_SSD_LAYERS, BATCH, SSD_HEADS, SSD_HEAD_DIM, D_STATE), jnp.float32)
    sconv0_p = jnp.zeros((N_SSD_LAYERS, BATCH, SSD_CONV - 1, SSD_CONV_DIM), fdt)
    pool0_p = jnp.zeros((N_POOL_LAYERS, BATCH, POOL_BUF, D_POOL), fdt)
    fconv0_p = jnp.zeros((DEPTH, BATCH, FFN_CONV - 1, D_FF), fdt)
    y_prompt, ssm_p, sconv_p, pool_p, fconv_p = run_trunk(x_prompt, 0, new_mem_k_prompt, new_mem_v_prompt,
                                                          ssm0_p, sconv0_p, pool0_p, fconv0_p, w)
    y_sample, ssm_s, sconv_s, pool_s, fconv_s = run_trunk(x_sample, PAST_LEN, cache_mem_k, cache_mem_v,
                                                          state_ssm, state_ssm_conv, state_pool, state_ffn_conv, w)
    return (y_prompt, y_sample, new_mem_k_prompt, new_mem_v_prompt, ssm_p, sconv_p, pool_p, fconv_p,
            ssm_s, sconv_s, pool_s, fconv_s)
```

```python
import functools

import jax
import jax.numpy as jnp
from jax import lax
from jax.experimental import pallas as pl
from jax.experimental.pallas import tpu as pltpu

F32 = jnp.float32
BF16 = jnp.bfloat16

EPS = 1e-6
PAST_LEN = 16384
POOL_WINDOWS = (2, 4, 8, 16)

SSD_HEAD_DIM = 64
SSD_HEADS = 64
SSD_GROUPS = 8
SSD_HPG = SSD_HEADS // SSD_GROUPS
D_STATE = 128
D_INNER = SSD_HEADS * SSD_HEAD_DIM
SSD_GDIM = D_INNER // SSD_GROUPS
SSD_BC = SSD_GROUPS * D_STATE
SSD_CONV_DIM = D_INNER + 2 * SSD_BC
SSD_CONV = 4
SSD_CHUNK = 128
HEADS_PAD = 128

XA_HEADS = 4
FFN_CONV = 3

SUBLANES_V7X = 8
VMEM_LIMIT_V7X = 50 * 1024 * 1024


def _cparams(*sem):
    return pltpu.CompilerParams(dimension_semantics=sem, vmem_limit_bytes=VMEM_LIMIT_V7X)


def _silu(x):
    return x * (1.0 / (1.0 + jnp.exp(-x)))


def _softplus(x):
    return jnp.maximum(x, 0.0) + jnp.log1p(jnp.exp(-jnp.abs(x)))


def _dot(a, b):
    return jnp.dot(a, b, preferred_element_type=F32)


def _dot_nt(a, b):
    return lax.dot_general(a, b, (((1,), (1,)), ((), ())), preferred_element_type=F32)


def _dot_tn(a, b):
    return lax.dot_general(a, b, (((0,), (0,)), ((), ())), preferred_element_type=F32)


def _split3(x):
    hi = x.astype(BF16)
    r1 = x - hi.astype(F32)
    mid = r1.astype(BF16)
    lo = (r1 - mid.astype(F32)).astype(BF16)
    return hi, mid, lo


def _dot_exact_rhs01(x, e01):
    hi, mid, lo = _split3(x)
    return _dot(hi, e01) + _dot(mid, e01) + _dot(lo, e01)


def _dot_exact_lhs01(e01, x):
    hi, mid, lo = _split3(x)
    return _dot(e01, hi) + _dot(e01, mid) + _dot(e01, lo)


def _head_expand_matrix(first_head, width):
    row = lax.broadcasted_iota(jnp.int32, (HEADS_PAD, width), 0)
    lane = lax.broadcasted_iota(jnp.int32, (HEADS_PAD, width), 1)
    return jnp.where(row == first_head + lane // SSD_HEAD_DIM, 1.0, 0.0).astype(BF16)


def _rms_kernel(x_ref, g_ref, o_ref):
    x = x_ref[...]
    ms = jnp.mean(x * x, axis=-1, keepdims=True)
    o_ref[...] = (x * lax.rsqrt(ms + EPS) * g_ref[...]).astype(o_ref.dtype)


def rmsnorm(x, g_stack, layer, out_dtype):
    m, d = x.shape
    tm = min(m, 512)
    g3 = g_stack.reshape(-1, 1, d)
    return pl.pallas_call(
        _rms_kernel,
        grid=(m // tm,),
        in_specs=[pl.BlockSpec((tm, d), lambda i: (i, 0)),
                  pl.BlockSpec((None, 1, d), lambda i: (layer, 0, 0))],
        out_specs=pl.BlockSpec((tm, d), lambda i: (i, 0)),
        out_shape=jax.ShapeDtypeStruct((m, d), out_dtype),
        compiler_params=_cparams("parallel"),
        name="rmsnorm",
    )(x, g3)


def _mm_kernel(x_ref, w_ref, *rest, has_res):
    o_ref = rest[-1]
    acc = _dot(x_ref[...], w_ref[...].astype(BF16))
    if has_res:
        acc = acc + rest[0][...]
    o_ref[...] = acc.astype(o_ref.dtype)


def matmul(x, w_stack, layer, *, ncols=None, res=None, out_dtype=F32, tn=512):
    m, k = x.shape
    n = ncols or w_stack.shape[-1]
    tm = min(m, 1024)
    tn = min(tn, n)
    assert m % tm == 0 and n % tn == 0
    in_specs = [pl.BlockSpec((tm, k), lambda i, j: (i, 0)),
                pl.BlockSpec((None, k, tn), lambda i, j: (layer, 0, j))]
    args = [x, w_stack]
    if res is not None:
        in_specs.append(pl.BlockSpec((tm, tn), lambda i, j: (i, j)))
        args.append(res)
    return pl.pallas_call(
        functools.partial(_mm_kernel, has_res=res is not None),
        grid=(m // tm, n // tn),
        in_specs=in_specs,
        out_specs=pl.BlockSpec((tm, tn), lambda i, j: (i, j)),
        out_shape=jax.ShapeDtypeStruct((m, n), out_dtype),
        compiler_params=_cparams("parallel", "arbitrary"),
        name="matmul",
    )(*args)


def _ssd_chunk_kernel(z0_ref, z1_ref, x0_ref, x1_ref, bc_ref, dtr_ref, dtrt_ref, cw_ref, cb_ref,
                      dtb_ref, dtbt_ref, alog_ref, alogt_ref, dskip_ref, gn_ref,
                      y_ref, h_ref, ext_ref, xbc_ref, yg_ref):
    q = x0_ref.shape[0]
    hist = SUBLANES_V7X
    c = pl.program_id(1)

    @pl.when(c == 0)
    def _():
        ext_ref[0:hist, :] = jnp.zeros((hist, SSD_CONV_DIM), F32)
        h_ref[...] = jnp.zeros(h_ref.shape, F32)

    half = D_INNER // 2
    ext_ref[hist:hist + q, 0:half] = x0_ref[...]
    ext_ref[hist:hist + q, half:D_INNER] = x1_ref[...]
    ext_ref[hist:hist + q, D_INNER:SSD_CONV_DIM] = bc_ref[...]

    for s in range(SSD_CONV_DIM // SSD_GDIM):
        cols = slice(s * SSD_GDIM, (s + 1) * SSD_GDIM)
        acc = cb_ref[:, cols]
        for k in range(SSD_CONV):
            acc = acc + ext_ref[pl.ds(hist - (SSD_CONV - 1) + k, q), cols] * cw_ref[k:k + 1, cols]
        xbc_ref[:, cols] = _silu(acc)
    ext_ref[0:hist, :] = ext_ref[q:q + hist, :]

    dt = _softplus(dtr_ref[...] + dtb_ref[...])
    da = dt * (-jnp.exp(alog_ref[...]))
    dt_t = _softplus(dtrt_ref[...] + dtbt_ref[...])
    da_t = dt_t * (-jnp.exp(alogt_ref[...]))
    ii = lax.broadcasted_iota(jnp.int32, (q, q), 0)
    jj = lax.broadcasted_iota(jnp.int32, (q, q), 1)
    causal = ii >= jj
    tri = jnp.where(causal, 1.0, 0.0).astype(BF16)
    tri_t = jnp.where(jj >= ii, 1.0, 0.0).astype(BF16)
    acum = _dot_exact_lhs01(tri, da)
    acum_t = _dot_exact_rhs01(da_t, tri_t)
    a_last = acum[q - 1:q, :]
    stack = jnp.concatenate([dt, jnp.exp(acum), jnp.exp(a_last - acum) * dt], axis=0)
    lane = lax.broadcasted_iota(jnp.int32, (q, 2 * SSD_HEAD_DIM), 1)
    first_of_pair = lane < SSD_HEAD_DIM

    for g in range(SSD_GROUPS):
        gcols = slice(g * SSD_GDIM, (g + 1) * SSD_GDIM)
        ex = _dot_exact_rhs01(stack, _head_expand_matrix(g * SSD_HPG, SSD_GDIM))
        dt_e, a_e, wl_e = ex[0:q], ex[q:2 * q], ex[2 * q:3 * q]
        xs = xbc_ref[:, gcols]
        bm = xbc_ref[:, D_INNER + g * D_STATE:D_INNER + (g + 1) * D_STATE].astype(BF16)
        cm = xbc_ref[:, D_INNER + SSD_BC + g * D_STATE:D_INNER + SSD_BC + (g + 1) * D_STATE].astype(BF16)
        xdt = (xs * dt_e).astype(BF16)
        xw = (xs * wl_e).astype(BF16)
        cbm = _dot_nt(cm, bm)
        hg = h_ref[g * SSD_HPG:(g + 1) * SSD_HPG].reshape(SSD_HPG * SSD_HEAD_DIM, D_STATE)
        y_off = _dot_nt(cm, hg.astype(BF16))
        s_new = _dot_tn(xw, bm)
        for pr in range(SSD_HPG // 2):
            pcols = slice(pr * 2 * SSD_HEAD_DIM, (pr + 1) * 2 * SSD_HEAD_DIM)
            ys = []
            for hh in range(2):
                h = g * SSD_HPG + pr * 2 + hh
                diff = acum[:, h:h + 1] - acum_t[h:h + 1, :]
                decay = jnp.exp(jnp.where(causal, diff, -jnp.inf))
                ys.append(_dot((cbm * decay).astype(BF16), xdt[:, pcols]))
            yg_ref[:, pcols] = jnp.where(first_of_pair, ys[0], ys[1]) + a_e[:, pcols] * y_off[:, pcols]
        for r in range(SSD_HPG):
            h = g * SSD_HPG + r
            rows = slice(r * SSD_HEAD_DIM, (r + 1) * SSD_HEAD_DIM)
            h_ref[h] = hg[rows] * jnp.exp(acum_t[h:h + 1, q - 1:q]) + s_new[rows]
        z_ref = z0_ref if g < SSD_GROUPS // 2 else z1_ref
        zc = slice((g % (SSD_GROUPS // 2)) * SSD_GDIM, (g % (SSD_GROUPS // 2) + 1) * SSD_GDIM)
        yv = (yg_ref[...] + xs * dskip_ref[:, gcols]) * _silu(z_ref[:, zc])
        ms = jnp.mean(yv * yv, axis=-1, keepdims=True)
        y_ref[:, gcols] = (yv * lax.rsqrt(ms + EPS) * gn_ref[:, gcols]).astype(y_ref.dtype)


def ssd_prompt(proj, dt_raw, bsz, layer, w):
    m = proj.shape[0]
    n = m // bsz
    q = SSD_CHUNK
    nc = n // q
    half = D_INNER // 2
    row = lambda b, c: b * nc + c
    pspec = lambda cb: pl.BlockSpec((q, half), lambda b, c: (row(b, c), cb))
    vec = lambda width: pl.BlockSpec((None, 1, width), lambda b, c: (layer, 0, 0))
    vec_t = pl.BlockSpec((None, HEADS_PAD, 1), lambda b, c: (layer, 0, 0))
    y, h = pl.pallas_call(
        _ssd_chunk_kernel,
        grid=(bsz, nc),
        in_specs=[pspec(0), pspec(1), pspec(2), pspec(3), pspec(4),
                  pl.BlockSpec((q, HEADS_PAD), lambda b, c: (row(b, c), 0)),
                  pl.BlockSpec((HEADS_PAD, q), lambda b, c: (0, row(b, c))),
                  pl.BlockSpec((None, SSD_CONV, SSD_CONV_DIM), lambda b, c: (layer, 0, 0)),
                  vec(SSD_CONV_DIM), vec(HEADS_PAD), vec_t, vec(HEADS_PAD), vec_t,
                  vec(D_INNER), vec(D_INNER)],
        out_specs=[pl.BlockSpec((q, D_INNER), lambda b, c: (row(b, c), 0)),
                   pl.BlockSpec((None, SSD_HEADS, SSD_HEAD_DIM, D_STATE), lambda b, c: (b, 0, 0, 0))],
        out_shape=[jax.ShapeDtypeStruct((m, D_INNER), BF16),
                   jax.ShapeDtypeStruct((bsz, SSD_HEADS, SSD_HEAD_DIM, D_STATE), F32)],
        scratch_shapes=[pltpu.VMEM((q + 2 * SUBLANES_V7X, SSD_CONV_DIM), F32),
                        pltpu.VMEM((q, SSD_CONV_DIM), F32),
                        pltpu.VMEM((q, SSD_GDIM), F32)],
        compiler_params=_cparams("parallel", "arbitrary"),
        name="ssd_chunk",
    )(proj, proj, proj, proj, proj, dt_raw, dt_raw.T, w["conv_w"], w["conv_b"],
      w["dt_bias"], w["dt_bias_t"], w["a_log"], w["a_log_t"], w["d_skip"], w["g_norm"])
    return y, h


def _conv_time_major(st_ref, x_ref, w_ref, b_ref, nb):
    width = w_ref.shape[0]
    hist = (width - 1) * nb
    rows = x_ref.shape[0]
    acc = b_ref[...]
    for k in range(width):
        parts = []
        if k * nb < hist:
            parts.append(st_ref[k * nb:hist, :])
        take = rows - (hist - k * nb)
        if take > 0:
            parts.append(x_ref[0:take, :])
        piece = parts[0] if len(parts) == 1 else jnp.concatenate(parts, axis=0)
        acc = acc + piece[0:rows] * w_ref[k:k + 1, :]
    return _silu(acc)


def _ssd_sample_prep_kernel(xr_ref, br_ref, cr_ref, sx_ref, sb_ref, sc_ref, wx_ref, wb_ref, wc_ref,
                            bx_ref, bb_ref, bcb_ref, dtr_ref, dtb_ref, alog_ref, dskip_ref,
                            y2_ref, ae_ref, xw_ref, bm_ref, cm_ref, dl_ref, *, nb, nt):
    g = pl.program_id(0)
    xs = _conv_time_major(sx_ref, xr_ref, wx_ref, bx_ref, nb)
    bm = _conv_time_major(sb_ref, br_ref, wb_ref, bb_ref, nb)
    cm = _conv_time_major(sc_ref, cr_ref, wc_ref, bcb_ref, nb)
    bm_ref[...] = bm.astype(BF16)
    cm_ref[...] = cm.astype(BF16)
    bmr = bm.astype(BF16).astype(F32)
    cmr = cm.astype(BF16).astype(F32)

    dt = _softplus(dtr_ref[...] + dtb_ref[...])
    da = dt * (-jnp.exp(alog_ref[...]))
    sl = lambda a, t: a[t * nb:(t + 1) * nb]
    acums = [sl(da, 0)]
    for t in range(1, nt):
        acums.append(acums[-1] + sl(da, t))
    dl_ref[...] = jnp.exp(acums[-1])

    row = lax.broadcasted_iota(jnp.int32, (HEADS_PAD, SSD_GDIM), 0)
    lane = lax.broadcasted_iota(jnp.int32, (HEADS_PAD, SSD_GDIM), 1)
    e01 = jnp.where(row == g * SSD_HPG + lane // SSD_HEAD_DIM, 1.0, 0.0).astype(BF16)
    ex = _dot_exact_rhs01(jnp.concatenate(acums + [dt], axis=0), e01)
    a_e = [sl(ex, t) for t in range(nt)]
    d_e = [sl(ex, nt + t) for t in range(nt)]
    for t in range(nt):
        y2 = sl(xs, t) * dskip_ref[...]
        for j in range(t + 1):
            cb = jnp.sum(sl(cmr, t) * sl(bmr, j), axis=-1, keepdims=True)
            y2 = y2 + jnp.exp(a_e[t] - a_e[j]) * d_e[j] * cb * sl(xs, j)
        y2_ref[t * nb:(t + 1) * nb, :] = y2
        ae_ref[t * nb:(t + 1) * nb, :] = jnp.exp(a_e[t])
        xw_ref[t * nb:(t + 1) * nb, :] = (jnp.exp(a_e[nt - 1] - a_e[t]) * d_e[t] * sl(xs, t)).astype(BF16)


def _ssd_sample_state_kernel(dl_ref, h0_ref, cm_ref, bm_ref, xw_ref, ae_ref, y2_ref, z_ref, gn_ref,
                             y_ref, h_ref):
    b = pl.program_id(0)
    for g in range(SSD_GROUPS):
        gcols = slice(g * SSD_GDIM, (g + 1) * SSD_GDIM)
        scols = slice(g * D_STATE, (g + 1) * D_STATE)
        hg = h0_ref[g * SSD_HPG:(g + 1) * SSD_HPG].reshape(SSD_HPG * SSD_HEAD_DIM, D_STATE)
        y_off = _dot_nt(cm_ref[:, scols], hg.astype(BF16))
        s_new = _dot_tn(xw_ref[:, gcols], bm_ref[:, scols])
        for r in range(SSD_HPG):
            h = g * SSD_HPG + r
            rows = slice(r * SSD_HEAD_DIM, (r + 1) * SSD_HEAD_DIM)
            h_ref[h] = hg[rows] * dl_ref[b, h] + s_new[rows]
        yv = (ae_ref[:, gcols] * y_off + y2_ref[:, gcols]) * _silu(z_ref[:, gcols])
        ms = jnp.mean(yv * yv, axis=-1, keepdims=True)
        y_ref[:, gcols] = (yv * lax.rsqrt(ms + EPS) * gn_ref[:, gcols]).astype(y_ref.dtype)


def _to_batch_major(a, nt, nb, pad_t):
    a = a.reshape(nt, nb, a.shape[-1]).transpose(1, 0, 2)
    return jnp.pad(a, ((0, 0), (0, pad_t - nt), (0, 0)))


def ssd_sample(proj, dt_raw, conv_state, h0_stack, nb, layer, w):
    m = proj.shape[0]
    nt = m // nb
    hist = (SSD_CONV - 1) * nb
    xblk = D_INNER // SSD_GDIM
    bblk = 2 * D_INNER // D_STATE
    cblk = bblk + SSD_GROUPS
    sbb = D_INNER // D_STATE
    scb = sbb + SSD_GROUPS
    vec = lambda width: pl.BlockSpec((None, 1, width), lambda g: (layer, 0, 0))
    tm_out = lambda width, dt: jax.ShapeDtypeStruct((m, width), dt)
    y2, a_e, xw, bm, cm, dlast = pl.pallas_call(
        functools.partial(_ssd_sample_prep_kernel, nb=nb, nt=nt),
        grid=(SSD_GROUPS,),
        in_specs=[pl.BlockSpec((m, SSD_GDIM), lambda g: (0, xblk + g)),
                  pl.BlockSpec((m, D_STATE), lambda g: (0, bblk + g)),
                  pl.BlockSpec((m, D_STATE), lambda g: (0, cblk + g)),
                  pl.BlockSpec((hist, SSD_GDIM), lambda g: (0, g)),
                  pl.BlockSpec((hist, D_STATE), lambda g: (0, sbb + g)),
                  pl.BlockSpec((hist, D_STATE), lambda g: (0, scb + g)),
                  pl.BlockSpec((None, SSD_CONV, SSD_GDIM), lambda g: (layer, 0, g)),
                  pl.BlockSpec((None, SSD_CONV, D_STATE), lambda g: (layer, 0, sbb + g)),
                  pl.BlockSpec((None, SSD_CONV, D_STATE), lambda g: (layer, 0, scb + g)),
                  pl.BlockSpec((None, 1, SSD_GDIM), lambda g: (layer, 0, g)),
                  pl.BlockSpec((None, 1, D_STATE), lambda g: (layer, 0, sbb + g)),
                  pl.BlockSpec((None, 1, D_STATE), lambda g: (layer, 0, scb + g)),
                  pl.BlockSpec((m, HEADS_PAD), lambda g: (0, 0)),
                  vec(HEADS_PAD), vec(HEADS_PAD),
                  pl.BlockSpec((None, 1, SSD_GDIM), lambda g: (layer, 0, g))],
        out_specs=[pl.BlockSpec((m, SSD_GDIM), lambda g: (0, g)),
                   pl.BlockSpec((m, SSD_GDIM), lambda g: (0, g)),
                   pl.BlockSpec((m, SSD_GDIM), lambda g: (0, g)),
                   pl.BlockSpec((m, D_STATE), lambda g: (0, g)),
                   pl.BlockSpec((m, D_STATE), lambda g: (0, g)),
                   pl.BlockSpec((nb, HEADS_PAD), lambda g: (0, 0))],
        out_shape=[tm_out(D_INNER, F32), tm_out(D_INNER, F32), tm_out(D_INNER, BF16),
                   tm_out(SSD_BC, BF16), tm_out(SSD_BC, BF16),
                   jax.ShapeDtypeStruct((nb, HEADS_PAD), F32)],
        compiler_params=_cparams("arbitrary"),
        name="ssd_sample_prep",
    )(proj, proj, proj, conv_state, conv_state, conv_state, w["conv_w"], w["conv_w"], w["conv_w"],
      w["conv_b"], w["conv_b"], w["conv_b"], dt_raw, w["dt_bias"], w["a_log"], w["d_skip"])

    tp = SUBLANES_V7X
    bmaj = lambda a: _to_batch_major(a, nt, nb, tp)
    per_b = lambda width: pl.BlockSpec((None, tp, width), lambda b: (b, 0, 0))
    y, h = pl.pallas_call(
        _ssd_sample_state_kernel,
        grid=(nb,),
        in_specs=[pl.BlockSpec(memory_space=pltpu.SMEM),
                  pl.BlockSpec((None, None, SSD_HEADS, SSD_HEAD_DIM, D_STATE), lambda b: (layer, b, 0, 0, 0)),
                  per_b(SSD_BC), per_b(SSD_BC), per_b(D_INNER), per_b(D_INNER), per_b(D_INNER), per_b(D_INNER),
                  pl.BlockSpec((None, 1, D_INNER), lambda b: (layer, 0, 0))],
        out_specs=[per_b(D_INNER),
                   pl.BlockSpec((None, SSD_HEADS, SSD_HEAD_DIM, D_STATE), lambda b: (b, 0, 0, 0))],
        out_shape=[jax.ShapeDtypeStruct((nb, tp, D_INNER), BF16),
                   jax.ShapeDtypeStruct((nb, SSD_HEADS, SSD_HEAD_DIM, D_STATE), F32)],
        compiler_params=_cparams("parallel"),
        name="ssd_sample_state",
    )(dlast[:, :SSD_HEADS], h0_stack, bmaj(cm), bmaj(bm), bmaj(xw), bmaj(a_e), bmaj(y2), bmaj(proj[:, :D_INNER]), w["g_norm"])
    y_tm = y[:, :nt].transpose(1, 0, 2).reshape(m, D_INNER)
    return y_tm, h


def _pool_kernel(*refs, tm, stride, halo, tail, tiles_per_seq, pos0, has_state):
    if has_state:
        u_ref, win_ref, wg_ref, sc_ref, st_ref, mixed_ref, vtail_ref, ext_ref = refs
    else:
        u_ref, win_ref, wg_ref, sc_ref, mixed_ref, vtail_ref, ext_ref, carry_ref = refs
    i = pl.program_id(0)
    g = pl.program_id(1)
    v = _dot(u_ref[...], win_ref[...].astype(BF16))
    width = v.shape[1]
    if has_state:
        ext_ref[0:halo, :] = st_ref[...]
    else:
        first = i % tiles_per_seq == 0

        @pl.when(first)
        def _():
            ext_ref[0:halo, :] = jnp.zeros((halo, width), F32)

        @pl.when(jnp.logical_not(first))
        def _():
            ext_ref[0:halo, :] = carry_ref[g]

        carry_ref[g] = v[tm - halo:tm]
    ext_ref[halo:halo + tm, :] = v
    vtail_ref[...] = v[tm - tail:tm]
    t0 = (i % tiles_per_seq) * (tm // stride)
    t = t0 + lax.broadcasted_iota(jnp.int32, (tm, width), 0) // stride

    for gi, win in enumerate(POOL_WINDOWS):
        @pl.when(g == gi)
        def _(win=win):
            s = v
            for d in range(1, win):
                s = s + ext_ref[pl.ds(halo - d * stride, tm), :]
            cnt = jnp.minimum(pos0 + t + 1, win).astype(F32)
            p = (s / cnt - v).astype(BF16)
            mixed_ref[...] = (_dot(p, wg_ref[...].astype(BF16)) * sc_ref[...]).astype(mixed_ref.dtype)


def pool_mix(u, w_in, w_grp, scale3, layer, *, n_seq, stride, pos0, state=None):
    m, d = u.shape
    ng = len(POOL_WINDOWS)
    gdim = w_in.shape[-1] // ng
    has_state = state is not None
    if has_state:
        tm, halo, tail, tiles_per_seq = m, state.shape[0], m, 1
    else:
        tm, halo, tail = min(m // n_seq, 1024), 2 * SUBLANES_V7X, 2 * SUBLANES_V7X
        tiles_per_seq = (m // n_seq) // tm
    in_specs = [pl.BlockSpec((tm, d), lambda i, g: (i, 0)),
                pl.BlockSpec((None, d, gdim), lambda i, g: (layer, 0, g)),
                pl.BlockSpec((None, None, gdim, gdim), lambda i, g: (layer, g, 0, 0)),
                pl.BlockSpec((None, 1, gdim), lambda i, g: (layer, 0, g))]
    args = [u, w_in, w_grp, scale3]
    scratch = [pltpu.VMEM((halo + tm, gdim), F32)]
    if has_state:
        in_specs.append(pl.BlockSpec((halo, gdim), lambda i, g: (0, g)))
        args.append(state)
    else:
        scratch.append(pltpu.VMEM((ng, halo, gdim), F32))
    return pl.pallas_call(
        functools.partial(_pool_kernel, tm=tm, stride=stride, halo=halo, tail=tail,
                          tiles_per_seq=tiles_per_seq, pos0=pos0, has_state=has_state),
        grid=(m // tm, ng),
        in_specs=in_specs,
        out_specs=[pl.BlockSpec((tm, gdim), lambda i, g: (i, g)),
                   pl.BlockSpec((tail, gdim), lambda i, g: (i, g))],
        out_shape=[jax.ShapeDtypeStruct((m, ng * gdim), BF16),
                   jax.ShapeDtypeStruct((m // tm * tail, ng * gdim), F32)],
        scratch_shapes=scratch,
        compiler_params=_cparams("arbitrary", "arbitrary"),
        name="pool_mix",
    )(*args)


def _attn_kernel(q_ref, k_ref, v_ref, o_ref):
    hd = q_ref.shape[-1] // XA_HEADS
    scale = hd ** -0.5
    for h in range(XA_HEADS):
        cols = slice(h * hd, (h + 1) * hd)
        s = _dot_nt(q_ref[:, cols], k_ref[:, cols].astype(BF16)) * scale
        e = jnp.exp(s - jnp.max(s, axis=-1, keepdims=True))
        p = e * (1.0 / jnp.sum(e, axis=-1, keepdims=True))
        o_ref[:, cols] = _dot(p.astype(BF16), v_ref[:, cols].astype(BF16)).astype(o_ref.dtype)


def attention(q, k, v, kv_layer):
    bsz, n, d = q.shape
    n_mem = k.shape[2]
    tq = min(n, 1024)
    kv_spec = pl.BlockSpec((None, None, n_mem, d), lambda b, i: (kv_layer, b, 0, 0))
    return pl.pallas_call(
        _attn_kernel,
        grid=(bsz, n // tq),
        in_specs=[pl.BlockSpec((None, tq, d), lambda b, i: (b, i, 0)), kv_spec, kv_spec],
        out_specs=pl.BlockSpec((None, tq, d), lambda b, i: (b, i, 0)),
        out_shape=jax.ShapeDtypeStruct((bsz, n, d), BF16),
        compiler_params=_cparams("parallel", "arbitrary"),
        name="mem_attention",
    )(q, k, v)


def _ffn_gate_kernel(*refs, tm, stride, halo, tail, tiles_per_seq, has_state):
    if has_state:
        u_ref, wg_ref, wu_ref, cw_ref, cb_ref, st_ref, h_ref, gtail_ref, ext_ref = refs
    else:
        u_ref, wg_ref, wu_ref, cw_ref, cb_ref, h_ref, gtail_ref, ext_ref, carry_ref = refs
    i = pl.program_id(0)
    j = pl.program_id(1)
    u = u_ref[...]
    gate = _dot(u, wg_ref[...].astype(BF16))
    up = _dot(u, wu_ref[...].astype(BF16))
    if has_state:
        ext_ref[0:halo, :] = st_ref[...]
    else:
        first = i % tiles_per_seq == 0

        @pl.when(first)
        def _():
            ext_ref[0:halo, :] = jnp.zeros((halo, gate.shape[1]), F32)

        @pl.when(jnp.logical_not(first))
        def _():
            ext_ref[0:halo, :] = carry_ref[j]

        carry_ref[j] = gate[tm - halo:tm]
    ext_ref[halo:halo + tm, :] = gate
    gtail_ref[...] = gate[tm - tail:tm]
    acc = cb_ref[...] + gate * cw_ref[FFN_CONV - 1:FFN_CONV, :]
    for k in range(FFN_CONV - 1):
        acc = acc + ext_ref[pl.ds(halo - (FFN_CONV - 1 - k) * stride, tm), :] * cw_ref[k:k + 1, :]
    h_ref[...] = (_silu(acc) * up).astype(h_ref.dtype)


def ffn_gate(u, w_gate, w_up, conv_w, conv_b3, layer, *, n_seq, stride, state=None, tn=512):
    m, d = u.shape
    nf = w_gate.shape[-1]
    has_state = state is not None
    if has_state:
        tm, halo, tail, tiles_per_seq = m, state.shape[0], state.shape[0], 1
    else:
        tm, halo, tail = min(m // n_seq, 1024), SUBLANES_V7X, SUBLANES_V7X
        tiles_per_seq = (m // n_seq) // tm
    wspec = pl.BlockSpec((None, d, tn), lambda i, j: (layer, 0, j))
    in_specs = [pl.BlockSpec((tm, d), lambda i, j: (i, 0)), wspec, wspec,
                pl.BlockSpec((None, FFN_CONV, tn), lambda i, j: (layer, 0, j)),
                pl.BlockSpec((None, 1, tn), lambda i, j: (layer, 0, j))]
    args = [u, w_gate, w_up, conv_w, conv_b3]
    scratch = [pltpu.VMEM((halo + tm, tn), F32)]
    if has_state:
        in_specs.append(pl.BlockSpec((halo, tn), lambda i, j: (0, j)))
        args.append(state)
    else:
        scratch.append(pltpu.VMEM((nf // tn, halo, tn), F32))
    return pl.pallas_call(
        functools.partial(_ffn_gate_kernel, tm=tm, stride=stride, halo=halo, tail=tail,
                          tiles_per_seq=tiles_per_seq, has_state=has_state),
        grid=(m // tm, nf // tn),
        in_specs=in_specs,
        out_specs=[pl.BlockSpec((tm, tn), lambda i, j: (i, j)),
                   pl.BlockSpec((tail, tn), lambda i, j: (i, j))],
        out_shape=[jax.ShapeDtypeStruct((m, nf), BF16),
                   jax.ShapeDtypeStruct((m // tm * tail, nf), F32)],
        scratch_shapes=scratch,
        compiler_params=_cparams("arbitrary", "arbitrary"),
        name="ffn_gate",
    )(*args)


def _ssd_weights(p):
    pad = HEADS_PAD - SSD_HEADS
    nl = p["ssd_dt_bias"].shape[0]
    padv = lambda a: jnp.pad(a, ((0, 0), (0, pad)))
    dt_bias = padv(p["ssd_dt_bias"])
    a_log = padv(p["ssd_a_log"])
    return dict(
        w_dt=jnp.pad(p["ssd_w_in"][:, :, D_INNER + SSD_CONV_DIM:], ((0, 0), (0, 0), (0, pad))),
        conv_w=p["ssd_conv_w"],
        conv_b=p["ssd_conv_b"].reshape(nl, 1, SSD_CONV_DIM),
        dt_bias=dt_bias.reshape(nl, 1, HEADS_PAD), dt_bias_t=dt_bias.reshape(nl, HEADS_PAD, 1),
        a_log=a_log.reshape(nl, 1, HEADS_PAD), a_log_t=a_log.reshape(nl, HEADS_PAD, 1),
        d_skip=jnp.repeat(p["ssd_d"], SSD_HEAD_DIM, axis=1).reshape(nl, 1, D_INNER),
        g_norm=p["ssd_norm"].reshape(nl, 1, D_INNER))


def _trunk(x, p, sw, *, prompt, bsz, mem_k, mem_v, state_ssm=None, state_ssm_conv=None, state_pool=None,
           state_ffn_conv=None):
    m, d = x.shape
    n = m // bsz
    depth = p["norm_mix"].shape[0]
    stride = 1 if prompt else bsz
    n_seq = bsz if prompt else 1
    proj_cols = D_INNER + SSD_CONV_DIM
    ssm_out, sconv_out, pool_out, fconv_out = [], [], [], []

    def time_major_state(s):
        return s.transpose(1, 0, 2).reshape(s.shape[1] * s.shape[0], s.shape[2])

    for i in range(depth):
        j = i // 2
        u = rmsnorm(x, p["norm_mix"], i, BF16)
        if i % 2 == 0:
            proj = matmul(u, p["ssd_w_in"], j, ncols=proj_cols)
            dt_raw = matmul(u, sw["w_dt"], j)
            if prompt:
                y, h = ssd_prompt(proj, dt_raw, bsz, j, sw)
                sconv_out.append(proj.reshape(bsz, n, proj_cols)[:, n - (SSD_CONV - 1):, D_INNER:])
            else:
                y, h = ssd_sample(proj, dt_raw, time_major_state(state_ssm_conv[j]), state_ssm, bsz, j, sw)
                xbc = proj.reshape(n, bsz, proj_cols)[:, :, D_INNER:]
                ext = jnp.concatenate([state_ssm_conv[j], xbc.transpose(1, 0, 2)], axis=1)
                sconv_out.append(ext[:, -(SSD_CONV - 1):])
            ssm_out.append(h)
            x = matmul(y, p["ssd_w_out"], j, res=x, tn=256)
        else:
            scale3 = p["pool_scale"].reshape(-1, 1, p["pool_scale"].shape[-1])
            if prompt:
                mixed, vtail = pool_mix(u, p["pool_w_in"], p["pool_w_grp"], scale3, j, n_seq=n_seq, stride=1,
                                        pos0=0)
                nbuf = max(POOL_WINDOWS) - 1
                pool_out.append(vtail.reshape(bsz, -1, vtail.shape[-1])[:, -nbuf:])
            else:
                mixed, vtail = pool_mix(u, p["pool_w_in"], p["pool_w_grp"], scale3, j, n_seq=n_seq, stride=bsz,
                                        pos0=PAST_LEN, state=time_major_state(state_pool[j]))
                v_bm = vtail.reshape(n, bsz, -1).transpose(1, 0, 2)
                nbuf = state_pool.shape[2]
                pool_out.append(jnp.concatenate([state_pool[j], v_bm], axis=1)[:, -nbuf:])
            x = matmul(mixed, p["pool_w_out"], j, res=x)

        u = rmsnorm(x, p["norm_xattn"], i, BF16)
        qm = matmul(u, p["xa_w_q"], i, out_dtype=BF16)
        if prompt:
            o = attention(qm.reshape(bsz, n, d), mem_k, mem_v, i).reshape(m, d)
        else:
            q3 = _to_batch_major(qm, n, bsz, SUBLANES_V7X)
            o = attention(q3, mem_k, mem_v, i)[:, :n].transpose(1, 0, 2).reshape(m, d)
        x = matmul(o, p["xa_w_o"], i, res=x)

        u = rmsnorm(x, p["norm_ffn"], i, BF16)
        cb3 = p["ffn_conv_b"].reshape(depth, 1, -1)
        if prompt:
            hcat, gtail = ffn_gate(u, p["ffn_w_gate"], p["ffn_w_up"], p["ffn_conv_w"], cb3, i, n_seq=n_seq,
                                   stride=1)
            fconv_out.append(gtail.reshape(bsz, -1, gtail.shape[-1])[:, -(FFN_CONV - 1):])
        else:
            hcat, gtail = ffn_gate(u, p["ffn_w_gate"], p["ffn_w_up"], p["ffn_conv_w"], cb3, i, n_seq=n_seq,
                                   stride=bsz, state=time_major_state(state_ffn_conv[i]))
            fconv_out.append(gtail.reshape(FFN_CONV - 1, bsz, -1).transpose(1, 0, 2))
        x = matmul(hcat, p["ffn_w_down"], i, res=x, tn=256)

    y = rmsnorm(x, p["norm_final"].reshape(1, d), 0, F32)
    return y, jnp.stack(ssm_out), jnp.stack(sconv_out), jnp.stack(pool_out), jnp.stack(fconv_out)


def kernel(x_prompt, x_sample, mem_prompt, cache_mem_k, cache_mem_v, state_ssm, state_ssm_conv, state_pool, state_ffn_conv, norm_mix, norm_xattn, norm_ffn, norm_mem, norm_final, ssd_w_in, ssd_conv_w, ssd_conv_b, ssd_dt_bias, ssd_a_log, ssd_d, ssd_norm, ssd_w_out, pool_w_in, pool_w_grp, pool_scale, pool_w_out, xa_w_q, xa_w_k, xa_w_v, xa_w_o, ffn_w_gate, ffn_conv_w, ffn_conv_b, ffn_w_up, ffn_w_down):
    p = dict(norm_mix=norm_mix, norm_xattn=norm_xattn, norm_ffn=norm_ffn, norm_final=norm_final,
             ssd_w_in=ssd_w_in, ssd_conv_w=ssd_conv_w, ssd_conv_b=ssd_conv_b, ssd_dt_bias=ssd_dt_bias,
             ssd_a_log=ssd_a_log, ssd_d=ssd_d, ssd_norm=ssd_norm, ssd_w_out=ssd_w_out,
             pool_w_in=pool_w_in, pool_w_grp=pool_w_grp, pool_scale=pool_scale, pool_w_out=pool_w_out,
             xa_w_q=xa_w_q, xa_w_o=xa_w_o, ffn_w_gate=ffn_w_gate, ffn_conv_w=ffn_conv_w,
             ffn_conv_b=ffn_conv_b, ffn_w_up=ffn_w_up, ffn_w_down=ffn_w_down)
    sw = _ssd_weights(p)
    depth = norm_mix.shape[0]
    bp, n_p, d = x_prompt.shape
    bs, n_s, _ = x_sample.shape
    n_mem = mem_prompt.shape[1]
    hd = d // XA_HEADS

    mem2 = mem_prompt.reshape(bp * n_mem, d)
    mk, mv = [], []
    for i in range(depth):
        mn = rmsnorm(mem2, norm_mem, i, BF16)
        mk.append(matmul(mn, xa_w_k, i).reshape(bp, n_mem, d))
        mv.append(matmul(mn, xa_w_v, i).reshape(bp, n_mem, d))
    mem_k_p = jnp.stack(mk)
    mem_v_p = jnp.stack(mv)

    y_p, ssm_p, sconv_p, pool_p, fconv_p = _trunk(
        x_prompt.reshape(bp * n_p, d), p, sw, prompt=True, bsz=bp, mem_k=mem_k_p, mem_v=mem_v_p)

    xs_tm = x_sample.transpose(1, 0, 2).reshape(n_s * bs, d)
    y_s, ssm_s, sconv_s, pool_s, fconv_s = _trunk(
        xs_tm, p, sw, prompt=False, bsz=bs,
        mem_k=cache_mem_k.reshape(depth, bs, n_mem, d), mem_v=cache_mem_v.reshape(depth, bs, n_mem, d),
        state_ssm=state_ssm, state_ssm_conv=state_ssm_conv, state_pool=state_pool,
        state_ffn_conv=state_ffn_conv)
    y_s = y_s.reshape(n_s, bs, d).transpose(1, 0, 2)

    kv_shape = (depth, bp, n_mem, XA_HEADS, hd)
    return (y_p.reshape(bp, n_p, d), y_s, mem_k_p.reshape(kv_shape), mem_v_p.reshape(kv_shape),
            ssm_p, sconv_p, pool_p, fconv_p, ssm_s, sconv_s, pool_s, fconv_s)
```

```python
import functools

import jax
import jax.numpy as jnp
from jax import lax
from jax.experimental import pallas as pl
from jax.experimental.pallas import tpu as pltpu

F32 = jnp.float32
BF16 = jnp.bfloat16

EPS = 1e-6
PAST_LEN = 16384
POOL_WINDOWS = (2, 4, 8, 16)

SSD_HEAD_DIM = 64
SSD_HEADS = 64
SSD_GROUPS = 8
SSD_HPG = SSD_HEADS // SSD_GROUPS
D_STATE = 128
D_INNER = SSD_HEADS * SSD_HEAD_DIM
SSD_GDIM = D_INNER // SSD_GROUPS
SSD_BC = SSD_GROUPS * D_STATE
SSD_CONV_DIM = D_INNER + 2 * SSD_BC
SSD_CONV = 4
SSD_CHUNK = 128
HEADS_PAD = 128

XA_HEADS = 4
FFN_CONV = 3

SUBLANES_V7X = 8
VMEM_LIMIT_V7X = 50 * 1024 * 1024


def _cparams(*sem):
    return pltpu.CompilerParams(dimension_semantics=sem, vmem_limit_bytes=VMEM_LIMIT_V7X)


def _silu(x):
    return x * (0.5 * jnp.tanh(0.5 * x) + 0.5)


def _softplus(x):
    return jnp.maximum(x, 0.0) + jnp.log1p(jnp.exp(-jnp.abs(x)))


def _dot(a, b):
    return jnp.dot(a, b, preferred_element_type=F32)


def _dot_nt(a, b):
    return lax.dot_general(a, b, (((1,), (1,)), ((), ())), preferred_element_type=F32)


def _dot_tn(a, b):
    return lax.dot_general(a, b, (((0,), (0,)), ((), ())), preferred_element_type=F32)


def _split3(x):
    hi = x.astype(BF16)
    r1 = x - hi.astype(F32)
    mid = r1.astype(BF16)
    lo = (r1 - mid.astype(F32)).astype(BF16)
    return hi, mid, lo


def _dot_split_rhs01(parts, e01):
    return _dot(parts[0], e01) + _dot(parts[1], e01) + _dot(parts[2], e01)


def _dot_exact_rhs01(x, e01):
    return _dot_split_rhs01(_split3(x), e01)


def _dot_exact_lhs01(e01, x):
    hi, mid, lo = _split3(x)
    return _dot(e01, hi) + _dot(e01, mid) + _dot(e01, lo)


def _head_expand_matrix(first_head, width):
    row = lax.broadcasted_iota(jnp.int32, (HEADS_PAD, width), 0)
    lane = lax.broadcasted_iota(jnp.int32, (HEADS_PAD, width), 1)
    return jnp.where(row == first_head + lane // SSD_HEAD_DIM, 1.0, 0.0).astype(BF16)


def _rms_kernel(x_ref, g_ref, o_ref):
    x = x_ref[...]
    ms = jnp.mean(x * x, axis=-1, keepdims=True)
    o_ref[...] = (x * lax.rsqrt(ms + EPS) * g_ref[...]).astype(o_ref.dtype)


def rmsnorm(x, g_stack, layer, out_dtype):
    m, d = x.shape
    tm = min(m, 512)
    g3 = g_stack.reshape(-1, 1, d)
    return pl.pallas_call(
        _rms_kernel,
        grid=(m // tm,),
        in_specs=[pl.BlockSpec((tm, d), lambda i: (i, 0)),
                  pl.BlockSpec((None, 1, d), lambda i: (layer, 0, 0))],
        out_specs=pl.BlockSpec((tm, d), lambda i: (i, 0)),
        out_shape=jax.ShapeDtypeStruct((m, d), out_dtype),
        compiler_params=_cparams("parallel"),
        name="rmsnorm",
    )(x, g3)


def _mm_kernel(x_ref, w_ref, *rest, has_res, w_is_nk):
    o_ref = rest[-1]
    w = w_ref[...].astype(BF16)
    acc = _dot_nt(x_ref[...], w) if w_is_nk else _dot(x_ref[...], w)
    if has_res:
        acc = acc + rest[0][...]
    o_ref[...] = acc.astype(o_ref.dtype)


def matmul(x, w_stack, layer, *, ncols=None, res=None, out_dtype=F32, tn=512, w_is_nk=False):
    m, k = x.shape
    n = ncols or w_stack.shape[1 if w_is_nk else 2]
    tm = 2048 if (k <= 2048 and res is None and m % 2048 == 0) else min(m, 1024)
    tn = min(tn, n)
    assert m % tm == 0 and n % tn == 0
    w_spec = (pl.BlockSpec((None, tn, k), lambda i, j: (layer, j, 0)) if w_is_nk
              else pl.BlockSpec((None, k, tn), lambda i, j: (layer, 0, j)))
    in_specs = [pl.BlockSpec((tm, k), lambda i, j: (i, 0)), w_spec]
    args = [x, w_stack]
    if res is not None:
        in_specs.append(pl.BlockSpec((tm, tn), lambda i, j: (i, j)))
        args.append(res)
    return pl.pallas_call(
        functools.partial(_mm_kernel, has_res=res is not None, w_is_nk=w_is_nk),
        grid=(m // tm, n // tn),
        in_specs=in_specs,
        out_specs=pl.BlockSpec((tm, tn), lambda i, j: (i, j)),
        out_shape=jax.ShapeDtypeStruct((m, n), out_dtype),
        compiler_params=_cparams("parallel", "arbitrary"),
        name="matmul",
    )(*args)


def _ssd_chunk_kernel(z0_ref, z1_ref, x0_ref, x1_ref, bc_ref, dtr_ref, dtrt_ref, cw_ref, cb_ref,
                      dtb_ref, dtbt_ref, alog_ref, alogt_ref, dskip_ref, gn_ref,
                      y_ref, h_ref, ext_ref, xbc_ref, yg_ref):
    q = x0_ref.shape[0]
    hist = SUBLANES_V7X
    c = pl.program_id(1)

    @pl.when(c == 0)
    def _():
        ext_ref[0:hist, :] = jnp.zeros((hist, SSD_CONV_DIM), F32)
        h_ref[...] = jnp.zeros(h_ref.shape, F32)

    half = D_INNER // 2
    ext_ref[hist:hist + q, 0:half] = x0_ref[...]
    ext_ref[hist:hist + q, half:D_INNER] = x1_ref[...]
    ext_ref[hist:hist + q, D_INNER:SSD_CONV_DIM] = bc_ref[...]

    for s in range(SSD_CONV_DIM // SSD_GDIM):
        cols = slice(s * SSD_GDIM, (s + 1) * SSD_GDIM)
        acc = cb_ref[:, cols]
        for k in range(SSD_CONV):
            acc = acc + ext_ref[pl.ds(hist - (SSD_CONV - 1) + k, q), cols] * cw_ref[k:k + 1, cols]
        xbc_ref[:, cols] = _silu(acc)
    ext_ref[0:hist, :] = ext_ref[q:q + hist, :]

    dt = _softplus(dtr_ref[...] + dtb_ref[...])
    da = dt * (-jnp.exp(alog_ref[...]))
    dt_t = _softplus(dtrt_ref[...] + dtbt_ref[...])
    da_t = dt_t * (-jnp.exp(alogt_ref[...]))
    ii = lax.broadcasted_iota(jnp.int32, (q, q), 0)
    jj = lax.broadcasted_iota(jnp.int32, (q, q), 1)
    causal = ii >= jj
    tri = jnp.where(causal, 1.0, 0.0).astype(BF16)
    tri_t = jnp.where(jj >= ii, 1.0, 0.0).astype(BF16)
    acum = _dot_exact_lhs01(tri, da)
    acum_t = _dot_exact_rhs01(da_t, tri_t)
    a_last = acum[q - 1:q, :]
    stack = _split3(jnp.concatenate([dt, jnp.exp(acum), jnp.exp(a_last - acum) * dt], axis=0))
    lane = lax.broadcasted_iota(jnp.int32, (q, 2 * SSD_HEAD_DIM), 1)
    first_of_pair = lane < SSD_HEAD_DIM

    for g in range(SSD_GROUPS):
        gcols = slice(g * SSD_GDIM, (g + 1) * SSD_GDIM)
        ex = _dot_split_rhs01(stack, _head_expand_matrix(g * SSD_HPG, SSD_GDIM))
        dt_e, a_e, wl_e = ex[0:q], ex[q:2 * q], ex[2 * q:3 * q]
        xs = xbc_ref[:, gcols]
        bm = xbc_ref[:, D_INNER + g * D_STATE:D_INNER + (g + 1) * D_STATE].astype(BF16)
        cm = xbc_ref[:, D_INNER + SSD_BC + g * D_STATE:D_INNER + SSD_BC + (g + 1) * D_STATE].astype(BF16)
        xdt = (xs * dt_e).astype(BF16)
        xw = (xs * wl_e).astype(BF16)
        cbm = _dot_nt(cm, bm)
        hg = h_ref[g * SSD_HPG:(g + 1) * SSD_HPG].reshape(SSD_HPG * SSD_HEAD_DIM, D_STATE)
        y_off = _dot_nt(cm, hg.astype(BF16))
        s_new = _dot_tn(xw, bm)
        for pr in range(SSD_HPG // 2):
            pcols = slice(pr * 2 * SSD_HEAD_DIM, (pr + 1) * 2 * SSD_HEAD_DIM)
            ys = []
            for hh in range(2):
                h = g * SSD_HPG + pr * 2 + hh
                diff = acum[:, h:h + 1] - acum_t[h:h + 1, :]
                decay = jnp.exp(jnp.where(causal, diff, -jnp.inf))
                ys.append(_dot((cbm * decay).astype(BF16), xdt[:, pcols]))
            yg_ref[:, pcols] = jnp.where(first_of_pair, ys[0], ys[1]) + a_e[:, pcols] * y_off[:, pcols]
        for r in range(SSD_HPG):
            h = g * SSD_HPG + r
            rows = slice(r * SSD_HEAD_DIM, (r + 1) * SSD_HEAD_DIM)
            h_ref[h] = hg[rows] * jnp.exp(acum_t[h:h + 1, q - 1:q]) + s_new[rows]
        z_ref = z0_ref if g < SSD_GROUPS // 2 else z1_ref
        zc = slice((g % (SSD_GROUPS // 2)) * SSD_GDIM, (g % (SSD_GROUPS // 2) + 1) * SSD_GDIM)
        yv = (yg_ref[...] + xs * dskip_ref[:, gcols]) * _silu(z_ref[:, zc])
        ms = jnp.mean(yv * yv, axis=-1, keepdims=True)
        y_ref[:, gcols] = (yv * lax.rsqrt(ms + EPS) * gn_ref[:, gcols]).astype(y_ref.dtype)


def ssd_prompt(proj, dt_raw, bsz, layer, w):
    m = proj.shape[0]
    n = m // bsz
    q = SSD_CHUNK
    nc = n // q
    half = D_INNER // 2
    row = lambda b, c: b * nc + c
    pspec = lambda cb: pl.BlockSpec((q, half), lambda b, c: (row(b, c), cb))
    vec = lambda width: pl.BlockSpec((None, 1, width), lambda b, c: (layer, 0, 0))
    vec_t = pl.BlockSpec((None, HEADS_PAD, 1), lambda b, c: (layer, 0, 0))
    y, h = pl.pallas_call(
        _ssd_chunk_kernel,
        grid=(bsz, nc),
        in_specs=[pspec(0), pspec(1), pspec(2), pspec(3), pspec(4),
                  pl.BlockSpec((q, HEADS_PAD), lambda b, c: (row(b, c), 0)),
                  pl.BlockSpec((HEADS_PAD, q), lambda b, c: (0, row(b, c))),
                  pl.BlockSpec((None, SSD_CONV, SSD_CONV_DIM), lambda b, c: (layer, 0, 0)),
                  vec(SSD_CONV_DIM), vec(HEADS_PAD), vec_t, vec(HEADS_PAD), vec_t,
                  vec(D_INNER), vec(D_INNER)],
        out_specs=[pl.BlockSpec((q, D_INNER), lambda b, c: (row(b, c), 0)),
                   pl.BlockSpec((None, SSD_HEADS, SSD_HEAD_DIM, D_STATE), lambda b, c: (b, 0, 0, 0))],
        out_shape=[jax.ShapeDtypeStruct((m, D_INNER), BF16),
                   jax.ShapeDtypeStruct((bsz, SSD_HEADS, SSD_HEAD_DIM, D_STATE), F32)],
        scratch_shapes=[pltpu.VMEM((q + 2 * SUBLANES_V7X, SSD_CONV_DIM), F32),
                        pltpu.VMEM((q, SSD_CONV_DIM), F32),
                        pltpu.VMEM((q, SSD_GDIM), F32)],
        compiler_params=_cparams("parallel", "arbitrary"),
        name="ssd_chunk",
    )(proj, proj, proj, proj, proj, dt_raw, dt_raw.T, w["conv_w"], w["conv_b"],
      w["dt_bias"], w["dt_bias_t"], w["a_log"], w["a_log_t"], w["d_skip"], w["g_norm"])
    return y, h


def _conv_time_major(st_ref, x_ref, w_ref, b_ref, nb):
    width = w_ref.shape[0]
    hist = (width - 1) * nb
    rows = x_ref.shape[0]
    acc = b_ref[...]
    for k in range(width):
        parts = []
        if k * nb < hist:
            parts.append(st_ref[k * nb:hist, :])
        take = rows - (hist - k * nb)
        if take > 0:
            parts.append(x_ref[0:take, :])
        piece = parts[0] if len(parts) == 1 else jnp.concatenate(parts, axis=0)
        acc = acc + piece[0:rows] * w_ref[k:k + 1, :]
    return _silu(acc)


def _ssd_sample_prep_kernel(xr_ref, br_ref, cr_ref, sx_ref, sb_ref, sc_ref, wx_ref, wb_ref, wc_ref,
                            bx_ref, bb_ref, bcb_ref, dtr_ref, dtb_ref, alog_ref, dskip_ref,
                            y2_ref, ae_ref, xw_ref, bm_ref, cm_ref, dl_ref, *, nb, nt):
    g = pl.program_id(0)
    xs = _conv_time_major(sx_ref, xr_ref, wx_ref, bx_ref, nb)
    bm = _conv_time_major(sb_ref, br_ref, wb_ref, bb_ref, nb)
    cm = _conv_time_major(sc_ref, cr_ref, wc_ref, bcb_ref, nb)
    bm_ref[...] = bm.astype(BF16)
    cm_ref[...] = cm.astype(BF16)
    bmr = bm.astype(BF16).astype(F32)
    cmr = cm.astype(BF16).astype(F32)

    dt = _softplus(dtr_ref[...] + dtb_ref[...])
    da = dt * (-jnp.exp(alog_ref[...]))
    sl = lambda a, t: a[t * nb:(t + 1) * nb]
    acums = [sl(da, 0)]
    for t in range(1, nt):
        acums.append(acums[-1] + sl(da, t))
    dl_ref[...] = jnp.exp(acums[-1])

    row = lax.broadcasted_iota(jnp.int32, (HEADS_PAD, SSD_GDIM), 0)
    lane = lax.broadcasted_iota(jnp.int32, (HEADS_PAD, SSD_GDIM), 1)
    e01 = jnp.where(row == g * SSD_HPG + lane // SSD_HEAD_DIM, 1.0, 0.0).astype(BF16)
    ex = _dot_exact_rhs01(jnp.concatenate(acums + [dt], axis=0), e01)
    a_e = [sl(ex, t) for t in range(nt)]
    d_e = [sl(ex, nt + t) for t in range(nt)]
    for t in range(nt):
        y2 = sl(xs, t) * dskip_ref[...]
        for j in range(t + 1):
            cb = jnp.sum(sl(cmr, t) * sl(bmr, j), axis=-1, keepdims=True)
            y2 = y2 + jnp.exp(a_e[t] - a_e[j]) * d_e[j] * cb * sl(xs, j)
        y2_ref[t * nb:(t + 1) * nb, :] = y2
        ae_ref[t * nb:(t + 1) * nb, :] = jnp.exp(a_e[t])
        xw_ref[t * nb:(t + 1) * nb, :] = (jnp.exp(a_e[nt - 1] - a_e[t]) * d_e[t] * sl(xs, t)).astype(BF16)


def _ssd_sample_state_kernel(dl_ref, h0_ref, cm_ref, bm_ref, xw_ref, ae_ref, y2_ref, z_ref, gn_ref,
                             y_ref, h_ref):
    b = pl.program_id(0)
    for g in range(SSD_GROUPS):
        gcols = slice(g * SSD_GDIM, (g + 1) * SSD_GDIM)
        scols = slice(g * D_STATE, (g + 1) * D_STATE)
        hg = h0_ref[g * SSD_HPG:(g + 1) * SSD_HPG].reshape(SSD_HPG * SSD_HEAD_DIM, D_STATE)
        y_off = _dot_nt(cm_ref[:, scols], hg.astype(BF16))
        s_new = _dot_tn(xw_ref[:, gcols], bm_ref[:, scols])
        for r in range(SSD_HPG):
            h = g * SSD_HPG + r
            rows = slice(r * SSD_HEAD_DIM, (r + 1) * SSD_HEAD_DIM)
            h_ref[h] = hg[rows] * dl_ref[b, h] + s_new[rows]
        yv = (ae_ref[:, gcols] * y_off + y2_ref[:, gcols]) * _silu(z_ref[:, gcols])
        ms = jnp.mean(yv * yv, axis=-1, keepdims=True)
        y_ref[:, gcols] = (yv * lax.rsqrt(ms + EPS) * gn_ref[:, gcols]).astype(y_ref.dtype)


def _to_batch_major(a, nt, nb, pad_t):
    a = a.reshape(nt, nb, a.shape[-1]).transpose(1, 0, 2)
    return jnp.pad(a, ((0, 0), (0, pad_t - nt), (0, 0)))


def ssd_sample(proj, dt_raw, conv_state, h0_stack, nb, layer, w):
    m = proj.shape[0]
    nt = m // nb
    hist = (SSD_CONV - 1) * nb
    xblk = D_INNER // SSD_GDIM
    bblk = 2 * D_INNER // D_STATE
    cblk = bblk + SSD_GROUPS
    sbb = D_INNER // D_STATE
    scb = sbb + SSD_GROUPS
    vec = lambda width: pl.BlockSpec((None, 1, width), lambda g: (layer, 0, 0))
    tm_out = lambda width, dt: jax.ShapeDtypeStruct((m, width), dt)
    y2, a_e, xw, bm, cm, dlast = pl.pallas_call(
        functools.partial(_ssd_sample_prep_kernel, nb=nb, nt=nt),
        grid=(SSD_GROUPS,),
        in_specs=[pl.BlockSpec((m, SSD_GDIM), lambda g: (0, xblk + g)),
                  pl.BlockSpec((m, D_STATE), lambda g: (0, bblk + g)),
                  pl.BlockSpec((m, D_STATE), lambda g: (0, cblk + g)),
                  pl.BlockSpec((hist, SSD_GDIM), lambda g: (0, g)),
                  pl.BlockSpec((hist, D_STATE), lambda g: (0, sbb + g)),
                  pl.BlockSpec((hist, D_STATE), lambda g: (0, scb + g)),
                  pl.BlockSpec((None, SSD_CONV, SSD_GDIM), lambda g: (layer, 0, g)),
                  pl.BlockSpec((None, SSD_CONV, D_STATE), lambda g: (layer, 0, sbb + g)),
                  pl.BlockSpec((None, SSD_CONV, D_STATE), lambda g: (layer, 0, scb + g)),
                  pl.BlockSpec((None, 1, SSD_GDIM), lambda g: (layer, 0, g)),
                  pl.BlockSpec((None, 1, D_STATE), lambda g: (layer, 0, sbb + g)),
                  pl.BlockSpec((None, 1, D_STATE), lambda g: (layer, 0, scb + g)),
                  pl.BlockSpec((m, HEADS_PAD), lambda g: (0, 0)),
                  vec(HEADS_PAD), vec(HEADS_PAD),
                  pl.BlockSpec((None, 1, SSD_GDIM), lambda g: (layer, 0, g))],
        out_specs=[pl.BlockSpec((m, SSD_GDIM), lambda g: (0, g)),
                   pl.BlockSpec((m, SSD_GDIM), lambda g: (0, g)),
                   pl.BlockSpec((m, SSD_GDIM), lambda g: (0, g)),
                   pl.BlockSpec((m, D_STATE), lambda g: (0, g)),
                   pl.BlockSpec((m, D_STATE), lambda g: (0, g)),
                   pl.BlockSpec((nb, HEADS_PAD), lambda g: (0, 0))],
        out_shape=[tm_out(D_INNER, F32), tm_out(D_INNER, F32), tm_out(D_INNER, BF16),
                   tm_out(SSD_BC, BF16), tm_out(SSD_BC, BF16),
                   jax.ShapeDtypeStruct((nb, HEADS_PAD), F32)],
        compiler_params=_cparams("arbitrary"),
        name="ssd_sample_prep",
    )(proj, proj, proj, conv_state, conv_state, conv_state, w["conv_w"], w["conv_w"], w["conv_w"],
      w["conv_b"], w["conv_b"], w["conv_b"], dt_raw, w["dt_bias"], w["a_log"], w["d_skip"])

    tp = SUBLANES_V7X
    bmaj = lambda a: _to_batch_major(a, nt, nb, tp)
    per_b = lambda width: pl.BlockSpec((None, tp, width), lambda b: (b, 0, 0))
    y, h = pl.pallas_call(
        _ssd_sample_state_kernel,
        grid=(nb,),
        in_specs=[pl.BlockSpec(memory_space=pltpu.SMEM),
                  pl.BlockSpec((None, None, SSD_HEADS, SSD_HEAD_DIM, D_STATE), lambda b: (layer, b, 0, 0, 0)),
                  per_b(SSD_BC), per_b(SSD_BC), per_b(D_INNER), per_b(D_INNER), per_b(D_INNER), per_b(D_INNER),
                  pl.BlockSpec((None, 1, D_INNER), lambda b: (layer, 0, 0))],
        out_specs=[per_b(D_INNER),
                   pl.BlockSpec((None, SSD_HEADS, SSD_HEAD_DIM, D_STATE), lambda b: (b, 0, 0, 0))],
        out_shape=[jax.ShapeDtypeStruct((nb, tp, D_INNER), BF16),
                   jax.ShapeDtypeStruct((nb, SSD_HEADS, SSD_HEAD_DIM, D_STATE), F32)],
        compiler_params=_cparams("parallel"),
        name="ssd_sample_state",
    )(dlast[:, :SSD_HEADS], h0_stack, bmaj(cm), bmaj(bm), bmaj(xw), bmaj(a_e), bmaj(y2), bmaj(proj[:, :D_INNER]), w["g_norm"])
    y_tm = y[:, :nt].transpose(1, 0, 2).reshape(m, D_INNER)
    return y_tm, h


def _pool_kernel(*refs, tm, stride, halo, tail, tiles_per_seq, pos0, has_state):
    if has_state:
        u_ref, win_ref, wg_ref, sc_ref, st_ref, mixed_ref, vtail_ref, ext_ref = refs
    else:
        u_ref, win_ref, wg_ref, sc_ref, mixed_ref, vtail_ref, ext_ref, carry_ref = refs
    i = pl.program_id(0)
    g = pl.program_id(1)
    v = _dot(u_ref[...], win_ref[...].astype(BF16))
    width = v.shape[1]
    if has_state:
        ext_ref[0:halo, :] = st_ref[...]
    else:
        first = i % tiles_per_seq == 0

        @pl.when(first)
        def _():
            ext_ref[0:halo, :] = jnp.zeros((halo, width), F32)

        @pl.when(jnp.logical_not(first))
        def _():
            ext_ref[0:halo, :] = carry_ref[g]

        carry_ref[g] = v[tm - halo:tm]
    ext_ref[halo:halo + tm, :] = v
    vtail_ref[...] = v[tm - tail:tm]
    t0 = (i % tiles_per_seq) * (tm // stride)
    t = t0 + lax.broadcasted_iota(jnp.int32, (tm, width), 0) // stride

    for gi, win in enumerate(POOL_WINDOWS):
        @pl.when(g == gi)
        def _(win=win):
            s = v
            for d in range(1, win):
                s = s + ext_ref[pl.ds(halo - d * stride, tm), :]
            cnt = jnp.minimum(pos0 + t + 1, win).astype(F32)
            p = (s / cnt - v).astype(BF16)
            mixed_ref[...] = (_dot(p, wg_ref[...].astype(BF16)) * sc_ref[...]).astype(mixed_ref.dtype)


def pool_mix(u, w_in, w_grp, scale3, layer, *, n_seq, stride, pos0, state=None):
    m, d = u.shape
    ng = len(POOL_WINDOWS)
    gdim = w_in.shape[-1] // ng
    has_state = state is not None
    if has_state:
        tm, halo, tail, tiles_per_seq = m, state.shape[0], m, 1
    else:
        tm, halo, tail = min(m // n_seq, 1024), 2 * SUBLANES_V7X, 2 * SUBLANES_V7X
        tiles_per_seq = (m // n_seq) // tm
    in_specs = [pl.BlockSpec((tm, d), lambda i, g: (i, 0)),
                pl.BlockSpec((None, d, gdim), lambda i, g: (layer, 0, g)),
                pl.BlockSpec((None, None, gdim, gdim), lambda i, g: (layer, g, 0, 0)),
                pl.BlockSpec((None, 1, gdim), lambda i, g: (layer, 0, g))]
    args = [u, w_in, w_grp, scale3]
    scratch = [pltpu.VMEM((halo + tm, gdim), F32)]
    if has_state:
        in_specs.append(pl.BlockSpec((halo, gdim), lambda i, g: (0, g)))
        args.append(state)
    else:
        scratch.append(pltpu.VMEM((ng, halo, gdim), F32))
    return pl.pallas_call(
        functools.partial(_pool_kernel, tm=tm, stride=stride, halo=halo, tail=tail,
                          tiles_per_seq=tiles_per_seq, pos0=pos0, has_state=has_state),
        grid=(m // tm, ng),
        in_specs=in_specs,
        out_specs=[pl.BlockSpec((tm, gdim), lambda i, g: (i, g)),
                   pl.BlockSpec((tail, gdim), lambda i, g: (i, g))],
        out_shape=[jax.ShapeDtypeStruct((m, ng * gdim), BF16),
                   jax.ShapeDtypeStruct((m // tm * tail, ng * gdim), F32)],
        scratch_shapes=scratch,
        compiler_params=_cparams("arbitrary", "arbitrary"),
        name="pool_mix",
    )(*args)


def _attn_kernel(q_ref, k_ref, v_ref, o_ref):
    hd = q_ref.shape[-1] // XA_HEADS
    scale = hd ** -0.5
    for h in range(XA_HEADS):
        cols = slice(h * hd, (h + 1) * hd)
        s = _dot_nt(q_ref[:, cols], k_ref[:, cols].astype(BF16)) * scale
        e = jnp.exp(s - jnp.max(s, axis=-1, keepdims=True))
        p = e * (1.0 / jnp.sum(e, axis=-1, keepdims=True))
        o_ref[:, cols] = _dot(p.astype(BF16), v_ref[:, cols].astype(BF16)).astype(o_ref.dtype)


def attention(q, k, v, kv_layer):
    bsz, n, d = q.shape
    n_mem = k.shape[2]
    tq = min(n, 1024)
    kv_spec = pl.BlockSpec((None, None, n_mem, d), lambda b, i: (kv_layer, b, 0, 0))
    return pl.pallas_call(
        _attn_kernel,
        grid=(bsz, n // tq),
        in_specs=[pl.BlockSpec((None, tq, d), lambda b, i: (b, i, 0)), kv_spec, kv_spec],
        out_specs=pl.BlockSpec((None, tq, d), lambda b, i: (b, i, 0)),
        out_shape=jax.ShapeDtypeStruct((bsz, n, d), BF16),
        compiler_params=_cparams("parallel", "arbitrary"),
        name="mem_attention",
    )(q, k, v)


def _attn_cache_kernel(q_ref, k_ref, v_ref, o_ref, *, tp):
    n_mem, heads, hd = k_ref.shape
    rows = n_mem * heads
    kf = k_ref[...].reshape(rows, hd).astype(BF16)
    vf = v_ref[...].reshape(rows, hd).astype(BF16)
    s_t = _dot_nt(kf, q_ref[...]) * (hd ** -0.5)
    r = lax.broadcasted_iota(jnp.int32, s_t.shape, 0)
    c = lax.broadcasted_iota(jnp.int32, s_t.shape, 1)
    s_t = jnp.where(r % heads == c // tp, s_t, -jnp.inf)
    e = jnp.exp(s_t - jnp.max(s_t, axis=0, keepdims=True))
    p_t = e * (1.0 / jnp.sum(e, axis=0, keepdims=True))
    o_ref[...] = _dot_tn(p_t.astype(BF16), vf).astype(o_ref.dtype)


def attention_cache(q, k, v, kv_layer, tp):
    bsz = q.shape[0]
    kv_spec = pl.BlockSpec((None, None) + k.shape[2:], lambda b: (kv_layer, b, 0, 0, 0))
    q_spec = pl.BlockSpec((None,) + q.shape[1:], lambda b: (b, 0, 0))
    return pl.pallas_call(
        functools.partial(_attn_cache_kernel, tp=tp),
        grid=(bsz,),
        in_specs=[q_spec, kv_spec, kv_spec],
        out_specs=q_spec,
        out_shape=jax.ShapeDtypeStruct(q.shape, BF16),
        compiler_params=_cparams("parallel"),
        name="cache_attention",
    )(q, k, v)


def _ffn_gate_kernel(*refs, tm, stride, halo, tail, tiles_per_seq, has_state):
    if has_state:
        u_ref, wg_ref, wu_ref, cw_ref, cb_ref, st_ref, h_ref, gtail_ref, ext_ref = refs
    else:
        u_ref, wg_ref, wu_ref, cw_ref, cb_ref, h_ref, gtail_ref, ext_ref, carry_ref = refs
    i = pl.program_id(0)
    j = pl.program_id(1)
    u = u_ref[...]
    if has_state:
        ext_ref[0:halo, :] = st_ref[...]
    else:
        first = i % tiles_per_seq == 0

        @pl.when(first)
        def _():
            ext_ref[0:halo, :] = jnp.zeros((halo, ext_ref.shape[1]), F32)

        @pl.when(jnp.logical_not(first))
        def _():
            ext_ref[0:halo, :] = carry_ref[j]

    gate = _dot(u, wg_ref[...].astype(BF16))
    up = _dot(u, wu_ref[...].astype(BF16))
    ext_ref[halo:halo + tm, :] = gate
    acc = cb_ref[...] + gate * cw_ref[FFN_CONV - 1:FFN_CONV, :]
    for k in range(FFN_CONV - 1):
        acc = acc + ext_ref[pl.ds(halo - (FFN_CONV - 1 - k) * stride, tm), :] * cw_ref[k:k + 1, :]
    h_ref[...] = (_silu(acc) * up).astype(h_ref.dtype)
    gtail_ref[...] = ext_ref[halo + tm - tail:halo + tm, :]
    if not has_state:
        carry_ref[j] = ext_ref[tm:tm + halo, :]


def ffn_gate(u, w_gate, w_up, conv_w, conv_b3, layer, *, n_seq, stride, state=None, tn=512):
    m, d = u.shape
    nf = w_gate.shape[-1]
    has_state = state is not None
    if has_state:
        tm, halo, tail, tiles_per_seq = m, state.shape[0], state.shape[0], 1
    else:
        tm, halo, tail = min(m // n_seq, 1024), SUBLANES_V7X, SUBLANES_V7X
        tiles_per_seq = (m // n_seq) // tm
    wspec = pl.BlockSpec((None, d, tn), lambda i, j: (layer, 0, j))
    in_specs = [pl.BlockSpec((tm, d), lambda i, j: (i, 0)), wspec, wspec,
                pl.BlockSpec((None, FFN_CONV, tn), lambda i, j: (layer, 0, j)),
                pl.BlockSpec((None, 1, tn), lambda i, j: (layer, 0, j))]
    args = [u, w_gate, w_up, conv_w, conv_b3]
    scratch = [pltpu.VMEM((halo + tm, tn), F32)]
    if has_state:
        in_specs.append(pl.BlockSpec((halo, tn), lambda i, j: (0, j)))
        args.append(state)
    else:
        scratch.append(pltpu.VMEM((nf // tn, halo, tn), F32))
    return pl.pallas_call(
        functools.partial(_ffn_gate_kernel, tm=tm, stride=stride, halo=halo, tail=tail,
                          tiles_per_seq=tiles_per_seq, has_state=has_state),
        grid=(m // tm, nf // tn),
        in_specs=in_specs,
        out_specs=[pl.BlockSpec((tm, tn), lambda i, j: (i, j)),
                   pl.BlockSpec((tail, tn), lambda i, j: (i, j))],
        out_shape=[jax.ShapeDtypeStruct((m, nf), BF16),
                   jax.ShapeDtypeStruct((m // tm * tail, nf), F32)],
        scratch_shapes=scratch,
        compiler_params=_cparams("arbitrary", "arbitrary"),
        name="ffn_gate",
    )(*args)


def _ssd_weights(p):
    pad = HEADS_PAD - SSD_HEADS
    nl = p["ssd_dt_bias"].shape[0]
    padv = lambda a: jnp.pad(a, ((0, 0), (0, pad)))
    dt_bias = padv(p["ssd_dt_bias"])
    a_log = padv(p["ssd_a_log"])
    return dict(
        w_in_nk=jnp.swapaxes(p["ssd_w_in"], 1, 2),
        w_dt=jnp.pad(p["ssd_w_in"][:, :, D_INNER + SSD_CONV_DIM:], ((0, 0), (0, 0), (0, pad))),
        conv_w=p["ssd_conv_w"],
        conv_b=p["ssd_conv_b"].reshape(nl, 1, SSD_CONV_DIM),
        dt_bias=dt_bias.reshape(nl, 1, HEADS_PAD), dt_bias_t=dt_bias.reshape(nl, HEADS_PAD, 1),
        a_log=a_log.reshape(nl, 1, HEADS_PAD), a_log_t=a_log.reshape(nl, HEADS_PAD, 1),
        d_skip=jnp.repeat(p["ssd_d"], SSD_HEAD_DIM, axis=1).reshape(nl, 1, D_INNER),
        g_norm=p["ssd_norm"].reshape(nl, 1, D_INNER))


def _trunk(x, p, sw, *, prompt, bsz, mem_k, mem_v, state_ssm=None, state_ssm_conv=None, state_pool=None,
           state_ffn_conv=None):
    m, d = x.shape
    n = m // bsz
    depth = p["norm_mix"].shape[0]
    stride = 1 if prompt else bsz
    n_seq = bsz if prompt else 1
    proj_cols = D_INNER + SSD_CONV_DIM
    ssm_out, sconv_out, pool_out, fconv_out = [], [], [], []

    def time_major_state(s):
        return s.transpose(1, 0, 2).reshape(s.shape[1] * s.shape[0], s.shape[2])

    for i in range(depth):
        j = i // 2
        u = rmsnorm(x, p["norm_mix"], i, BF16)
        if i % 2 == 0:
            proj = matmul(u, sw["w_in_nk"], j, ncols=proj_cols, w_is_nk=True)
            dt_raw = matmul(u, sw["w_dt"], j)
            if prompt:
                y, h = ssd_prompt(proj, dt_raw, bsz, j, sw)
                sconv_out.append(proj.reshape(bsz, n, proj_cols)[:, n - (SSD_CONV - 1):, D_INNER:])
            else:
                y, h = ssd_sample(proj, dt_raw, time_major_state(state_ssm_conv[j]), state_ssm, bsz, j, sw)
                xbc = proj.reshape(n, bsz, proj_cols)[:, :, D_INNER:]
                ext = jnp.concatenate([state_ssm_conv[j], xbc.transpose(1, 0, 2)], axis=1)
                sconv_out.append(ext[:, -(SSD_CONV - 1):])
            ssm_out.append(h)
            x = matmul(y, p["ssd_w_out"], j, res=x, tn=256)
        else:
            scale3 = p["pool_scale"].reshape(-1, 1, p["pool_scale"].shape[-1])
            if prompt:
                mixed, vtail = pool_mix(u, p["pool_w_in"], p["pool_w_grp"], scale3, j, n_seq=n_seq, stride=1,
                                        pos0=0)
                nbuf = max(POOL_WINDOWS) - 1
                pool_out.append(vtail.reshape(bsz, -1, vtail.shape[-1])[:, -nbuf:])
            else:
                mixed, vtail = pool_mix(u, p["pool_w_in"], p["pool_w_grp"], scale3, j, n_seq=n_seq, stride=bsz,
                                        pos0=PAST_LEN, state=time_major_state(state_pool[j]))
                v_bm = vtail.reshape(n, bsz, -1).transpose(1, 0, 2)
                nbuf = state_pool.shape[2]
                pool_out.append(jnp.concatenate([state_pool[j], v_bm], axis=1)[:, -nbuf:])
            x = matmul(mixed, p["pool_w_out"], j, res=x)

        u = rmsnorm(x, p["norm_xattn"], i, BF16)
        qm = matmul(u, p["xa_w_q"], i, out_dtype=BF16)
        if prompt:
            o = attention(qm.reshape(bsz, n, d), mem_k, mem_v, i).reshape(m, d)
        else:
            tp = SUBLANES_V7X
            hd = d // XA_HEADS
            q4 = qm.reshape(n, bsz, XA_HEADS, hd).transpose(1, 2, 0, 3)
            q4 = jnp.pad(q4, ((0, 0), (0, 0), (0, tp - n), (0, 0))).reshape(bsz, XA_HEADS * tp, hd)
            o4 = attention_cache(q4, mem_k, mem_v, i, tp).reshape(bsz, XA_HEADS, tp, hd)[:, :, :n]
            o = o4.transpose(2, 0, 1, 3).reshape(m, d)
        x = matmul(o, p["xa_w_o"], i, res=x)

        u = rmsnorm(x, p["norm_ffn"], i, BF16)
        cb3 = p["ffn_conv_b"].reshape(depth, 1, -1)
        if prompt:
            hcat, gtail = ffn_gate(u, p["ffn_w_gate"], p["ffn_w_up"], p["ffn_conv_w"], cb3, i, n_seq=n_seq,
                                   stride=1)
            fconv_out.append(gtail.reshape(bsz, -1, gtail.shape[-1])[:, -(FFN_CONV - 1):])
        else:
            hcat, gtail = ffn_gate(u, p["ffn_w_gate"], p["ffn_w_up"], p["ffn_conv_w"], cb3, i, n_seq=n_seq,
                                   stride=bsz, state=time_major_state(state_ffn_conv[i]))
            fconv_out.append(gtail.reshape(FFN_CONV - 1, bsz, -1).transpose(1, 0, 2))
        x = matmul(hcat, p["ffn_w_down"], i, res=x, tn=256)

    y = rmsnorm(x, p["norm_final"].reshape(1, d), 0, F32)
    return y, jnp.stack(ssm_out), jnp.stack(sconv_out), jnp.stack(pool_out), jnp.stack(fconv_out)


def kernel(x_prompt, x_sample, mem_prompt, cache_mem_k, cache_mem_v, state_ssm, state_ssm_conv, state_pool, state_ffn_conv, norm_mix, norm_xattn, norm_ffn, norm_mem, norm_final, ssd_w_in, ssd_conv_w, ssd_conv_b, ssd_dt_bias, ssd_a_log, ssd_d, ssd_norm, ssd_w_out, pool_w_in, pool_w_grp, pool_scale, pool_w_out, xa_w_q, xa_w_k, xa_w_v, xa_w_o, ffn_w_gate, ffn_conv_w, ffn_conv_b, ffn_w_up, ffn_w_down):
    p = dict(norm_mix=norm_mix, norm_xattn=norm_xattn, norm_ffn=norm_ffn, norm_final=norm_final,
             ssd_w_in=ssd_w_in, ssd_conv_w=ssd_conv_w, ssd_conv_b=ssd_conv_b, ssd_dt_bias=ssd_dt_bias,
             ssd_a_log=ssd_a_log, ssd_d=ssd_d, ssd_norm=ssd_norm, ssd_w_out=ssd_w_out,
             pool_w_in=pool_w_in, pool_w_grp=pool_w_grp, pool_scale=pool_scale, pool_w_out=pool_w_out,
             xa_w_q=xa_w_q, xa_w_o=xa_w_o, ffn_w_gate=ffn_w_gate, ffn_conv_w=ffn_conv_w,
             ffn_conv_b=ffn_conv_b, ffn_w_up=ffn_w_up, ffn_w_down=ffn_w_down)
    sw = _ssd_weights(p)
    depth = norm_mix.shape[0]
    bp, n_p, d = x_prompt.shape
    bs, n_s, _ = x_sample.shape
    n_mem = mem_prompt.shape[1]
    hd = d // XA_HEADS

    mem2 = mem_prompt.reshape(bp * n_mem, d)
    mk, mv = [], []
    for i in range(depth):
        mn = rmsnorm(mem2, norm_mem, i, BF16)
        mk.append(matmul(mn, xa_w_k, i).reshape(bp, n_mem, d))
        mv.append(matmul(mn, xa_w_v, i).reshape(bp, n_mem, d))
    mem_k_p = jnp.stack(mk)
    mem_v_p = jnp.stack(mv)

    y_p, ssm_p, sconv_p, pool_p, fconv_p = _trunk(
        x_prompt.reshape(bp * n_p, d), p, sw, prompt=True, bsz=bp, mem_k=mem_k_p, mem_v=mem_v_p)

    xs_tm = x_sample.transpose(1, 0, 2).reshape(n_s * bs, d)
    y_s, ssm_s, sconv_s, pool_s, fconv_s = _trunk(
        xs_tm, p, sw, prompt=False, bsz=bs,
        mem_k=cache_mem_k, mem_v=cache_mem_v,
        state_ssm=state_ssm, state_ssm_conv=state_ssm_conv, state_pool=state_pool,
        state_ffn_conv=state_ffn_conv)
    y_s = y_s.reshape(n_s, bs, d).transpose(1, 0, 2)

    kv_shape = (depth, bp, n_mem, XA_HEADS, hd)
    return (y_p.reshape(bp, n_p, d), y_s, mem_k_p.reshape(kv_shape), mem_v_p.reshape(kv_shape),
            ssm_p, sconv_p, pool_p, fconv_p, ssm_s, sconv_s, pool_s, fconv_s)
```

```python
import functools

import jax
import jax.numpy as jnp
from jax import lax
from jax.experimental import pallas as pl
from jax.experimental.pallas import tpu as pltpu

F32 = jnp.float32
BF16 = jnp.bfloat16

EPS = 1e-6
PAST_LEN = 16384
POOL_WINDOWS = (2, 4, 8, 16)

SSD_HEAD_DIM = 64
SSD_HEADS = 64
SSD_GROUPS = 8
SSD_HPG = SSD_HEADS // SSD_GROUPS
D_STATE = 128
D_INNER = SSD_HEADS * SSD_HEAD_DIM
SSD_GDIM = D_INNER // SSD_GROUPS
SSD_BC = SSD_GROUPS * D_STATE
SSD_CONV_DIM = D_INNER + 2 * SSD_BC
SSD_CONV = 4
SSD_CHUNK = 128
HEADS_PAD = 128

XA_HEADS = 4
FFN_CONV = 3

SUBLANES_V7X = 8
VMEM_LIMIT_V7X = 50 * 1024 * 1024


def _cparams(*sem):
    return pltpu.CompilerParams(dimension_semantics=sem, vmem_limit_bytes=VMEM_LIMIT_V7X)


def _silu(x):
    return x * (0.5 * jnp.tanh(0.5 * x) + 0.5)


def _softplus(x):
    return jnp.maximum(x, 0.0) + jnp.log1p(jnp.exp(-jnp.abs(x)))


def _dot(a, b):
    return jnp.dot(a, b, preferred_element_type=F32)


def _dot_nt(a, b):
    return lax.dot_general(a, b, (((1,), (1,)), ((), ())), preferred_element_type=F32)


def _dot_tn(a, b):
    return lax.dot_general(a, b, (((0,), (0,)), ((), ())), preferred_element_type=F32)


def _split3(x):
    hi = x.astype(BF16)
    r1 = x - hi.astype(F32)
    mid = r1.astype(BF16)
    lo = (r1 - mid.astype(F32)).astype(BF16)
    return hi, mid, lo


def _dot_split_rhs01(parts, e01):
    return _dot(parts[0], e01) + _dot(parts[1], e01) + _dot(parts[2], e01)


def _dot_exact_rhs01(x, e01):
    return _dot_split_rhs01(_split3(x), e01)


def _dot_exact_lhs01(e01, x):
    hi, mid, lo = _split3(x)
    return _dot(e01, hi) + _dot(e01, mid) + _dot(e01, lo)


def _head_expand_matrix(first_head, width):
    row = lax.broadcasted_iota(jnp.int32, (HEADS_PAD, width), 0)
    lane = lax.broadcasted_iota(jnp.int32, (HEADS_PAD, width), 1)
    return jnp.where(row == first_head + lane // SSD_HEAD_DIM, 1.0, 0.0).astype(BF16)


def _rms_kernel(x_ref, g_ref, o_ref):
    x = x_ref[...]
    ms = jnp.mean(x * x, axis=-1, keepdims=True)
    o_ref[...] = (x * lax.rsqrt(ms + EPS) * g_ref[...]).astype(o_ref.dtype)


def rmsnorm(x, g_stack, layer, out_dtype):
    m, d = x.shape
    tm = min(m, 512)
    g3 = g_stack.reshape(-1, 1, d)
    return pl.pallas_call(
        _rms_kernel,
        grid=(m // tm,),
        in_specs=[pl.BlockSpec((tm, d), lambda i: (i, 0)),
                  pl.BlockSpec((None, 1, d), lambda i: (layer, 0, 0))],
        out_specs=pl.BlockSpec((tm, d), lambda i: (i, 0)),
        out_shape=jax.ShapeDtypeStruct((m, d), out_dtype),
        compiler_params=_cparams("parallel"),
        name="rmsnorm",
    )(x, g3)


def _mm_kernel(x_ref, w_ref, *rest, has_res, w_is_nk):
    o_ref = rest[-1]
    w = w_ref[...].astype(BF16)
    acc = _dot_nt(x_ref[...], w) if w_is_nk else _dot(x_ref[...], w)
    if has_res:
        acc = acc + rest[0][...]
    o_ref[...] = acc.astype(o_ref.dtype)


def matmul(x, w_stack, layer, *, ncols=None, res=None, out_dtype=F32, tn=512, w_is_nk=False):
    m, k = x.shape
    n = ncols or w_stack.shape[1 if w_is_nk else 2]
    tm = 2048 if (k <= 2048 and m % 2048 == 0) else min(m, 1024)
    tn = min(tn, n)
    assert m % tm == 0 and n % tn == 0
    w_spec = (pl.BlockSpec((None, tn, k), lambda i, j: (layer, j, 0)) if w_is_nk
              else pl.BlockSpec((None, k, tn), lambda i, j: (layer, 0, j)))
    in_specs = [pl.BlockSpec((tm, k), lambda i, j: (i, 0)), w_spec]
    args = [x, w_stack]
    if res is not None:
        in_specs.append(pl.BlockSpec((tm, tn), lambda i, j: (i, j)))
        args.append(res)
    return pl.pallas_call(
        functools.partial(_mm_kernel, has_res=res is not None, w_is_nk=w_is_nk),
        grid=(m // tm, n // tn),
        in_specs=in_specs,
        out_specs=pl.BlockSpec((tm, tn), lambda i, j: (i, j)),
        out_shape=jax.ShapeDtypeStruct((m, n), out_dtype),
        compiler_params=_cparams("parallel", "arbitrary"),
        name="matmul",
    )(*args)


def _mem_kv_kernel(x_ref, g_ref, wk_ref, wv_ref, k_ref, v_ref, xn_ref):
    @pl.when(pl.program_id(1) == 0)
    def _():
        x = x_ref[...]
        ms = jnp.mean(x * x, axis=-1, keepdims=True)
        xn_ref[...] = (x * lax.rsqrt(ms + EPS) * g_ref[...]).astype(BF16)

    xn = xn_ref[...]
    k_ref[...] = _dot(xn, wk_ref[...].astype(BF16))
    v_ref[...] = _dot(xn, wv_ref[...].astype(BF16))


def mem_kv(mem, g_stack, w_k, w_v, tn=256):
    m, d = mem.shape
    nl = w_k.shape[0]
    w_spec = pl.BlockSpec((None, d, tn), lambda l, j: (l, 0, j))
    o_spec = pl.BlockSpec((None, m, tn), lambda l, j: (l, 0, j))
    out = jax.ShapeDtypeStruct((nl, m, d), F32)
    return pl.pallas_call(
        _mem_kv_kernel,
        grid=(nl, d // tn),
        in_specs=[pl.BlockSpec((m, d), lambda l, j: (0, 0)),
                  pl.BlockSpec((None, 1, d), lambda l, j: (l, 0, 0)), w_spec, w_spec],
        out_specs=[o_spec, o_spec],
        out_shape=[out, out],
        scratch_shapes=[pltpu.VMEM((m, d), BF16)],
        compiler_params=_cparams("arbitrary", "arbitrary"),
        name="mem_kv",
    )(mem, g_stack.reshape(nl, 1, d), w_k, w_v)


def _ssd_chunk_kernel(z0_ref, z1_ref, x0_ref, x1_ref, bc_ref, dtr_ref, dtrt_ref, cw_ref, cb_ref,
                      dtb_ref, dtbt_ref, alog_ref, alogt_ref, dskip_ref, gn_ref,
                      y_ref, h_ref, ext_ref, xbc_ref, yg_ref):
    q = x0_ref.shape[0]
    hist = SUBLANES_V7X
    c = pl.program_id(1)

    @pl.when(c == 0)
    def _():
        ext_ref[0:hist, :] = jnp.zeros((hist, SSD_CONV_DIM), F32)
        h_ref[...] = jnp.zeros(h_ref.shape, F32)

    half = D_INNER // 2
    ext_ref[hist:hist + q, 0:half] = x0_ref[...]
    ext_ref[hist:hist + q, half:D_INNER] = x1_ref[...]
    ext_ref[hist:hist + q, D_INNER:SSD_CONV_DIM] = bc_ref[...]

    for s in range(SSD_CONV_DIM // SSD_GDIM):
        cols = slice(s * SSD_GDIM, (s + 1) * SSD_GDIM)
        full = ext_ref[:, cols]
        acc = cb_ref[:, cols] + full[hist:hist + q] * cw_ref[SSD_CONV - 1:SSD_CONV, cols]
        for k in range(SSD_CONV - 1):
            shifted = pltpu.roll(full, SSD_CONV - 1 - k, axis=0)
            acc = acc + shifted[hist:hist + q] * cw_ref[k:k + 1, cols]
        xbc_ref[:, cols] = _silu(acc)
    ext_ref[0:hist, :] = ext_ref[q:q + hist, :]

    dt = _softplus(dtr_ref[...] + dtb_ref[...])
    da = dt * (-jnp.exp(alog_ref[...]))
    dt_t = _softplus(dtrt_ref[...] + dtbt_ref[...])
    da_t = dt_t * (-jnp.exp(alogt_ref[...]))
    ii = lax.broadcasted_iota(jnp.int32, (q, q), 0)
    jj = lax.broadcasted_iota(jnp.int32, (q, q), 1)
    causal = ii >= jj
    tri = jnp.where(causal, 1.0, 0.0).astype(BF16)
    tri_t = jnp.where(jj >= ii, 1.0, 0.0).astype(BF16)
    acum = _dot_exact_lhs01(tri, da)
    acum_t = _dot_exact_rhs01(da_t, tri_t)
    a_last = acum[q - 1:q, :]
    stack = _split3(jnp.concatenate([dt, jnp.exp(acum), jnp.exp(a_last - acum) * dt], axis=0))
    lane = lax.broadcasted_iota(jnp.int32, (q, 2 * SSD_HEAD_DIM), 1)
    first_of_pair = lane < SSD_HEAD_DIM

    for g in range(SSD_GROUPS):
        gcols = slice(g * SSD_GDIM, (g + 1) * SSD_GDIM)
        ex = _dot_split_rhs01(stack, _head_expand_matrix(g * SSD_HPG, SSD_GDIM))
        dt_e, a_e, wl_e = ex[0:q], ex[q:2 * q], ex[2 * q:3 * q]
        xs = xbc_ref[:, gcols]
        bm = xbc_ref[:, D_INNER + g * D_STATE:D_INNER + (g + 1) * D_STATE].astype(BF16)
        cm = xbc_ref[:, D_INNER + SSD_BC + g * D_STATE:D_INNER + SSD_BC + (g + 1) * D_STATE].astype(BF16)
        xdt = (xs * dt_e).astype(BF16)
        xw = (xs * wl_e).astype(BF16)
        cbm = _dot_nt(cm, bm)
        hg = h_ref[g * SSD_HPG:(g + 1) * SSD_HPG].reshape(SSD_HPG * SSD_HEAD_DIM, D_STATE)
        y_off = _dot_nt(cm, hg.astype(BF16))
        s_new = _dot_tn(xw, bm)
        for pr in range(SSD_HPG // 2):
            pcols = slice(pr * 2 * SSD_HEAD_DIM, (pr + 1) * 2 * SSD_HEAD_DIM)
            ys = []
            for hh in range(2):
                h = g * SSD_HPG + pr * 2 + hh
                diff = acum[:, h:h + 1] - acum_t[h:h + 1, :]
                decay = jnp.exp(jnp.where(causal, diff, -jnp.inf))
                ys.append(_dot((cbm * decay).astype(BF16), xdt[:, pcols]))
            yg_ref[:, pcols] = jnp.where(first_of_pair, ys[0], ys[1]) + a_e[:, pcols] * y_off[:, pcols]
        for r in range(SSD_HPG):
            h = g * SSD_HPG + r
            rows = slice(r * SSD_HEAD_DIM, (r + 1) * SSD_HEAD_DIM)
            h_ref[h] = hg[rows] * jnp.exp(acum_t[h:h + 1, q - 1:q]) + s_new[rows]
        z_ref = z0_ref if g < SSD_GROUPS // 2 else z1_ref
        zc = slice((g % (SSD_GROUPS // 2)) * SSD_GDIM, (g % (SSD_GROUPS // 2) + 1) * SSD_GDIM)
        yv = (yg_ref[...] + xs * dskip_ref[:, gcols]) * _silu(z_ref[:, zc])
        ms = jnp.mean(yv * yv, axis=-1, keepdims=True)
        y_ref[:, gcols] = (yv * lax.rsqrt(ms + EPS) * gn_ref[:, gcols]).astype(y_ref.dtype)


def ssd_prompt(proj, dt_raw, bsz, layer, w):
    m = proj.shape[0]
    n = m // bsz
    q = SSD_CHUNK
    nc = n // q
    half = D_INNER // 2
    row = lambda b, c: b * nc + c
    pspec = lambda cb: pl.BlockSpec((q, half), lambda b, c: (row(b, c), cb))
    vec = lambda width: pl.BlockSpec((None, 1, width), lambda b, c: (layer, 0, 0))
    vec_t = pl.BlockSpec((None, HEADS_PAD, 1), lambda b, c: (layer, 0, 0))
    y, h = pl.pallas_call(
        _ssd_chunk_kernel,
        grid=(bsz, nc),
        in_specs=[pspec(0), pspec(1), pspec(2), pspec(3), pspec(4),
                  pl.BlockSpec((q, HEADS_PAD), lambda b, c: (row(b, c), 0)),
                  pl.BlockSpec((HEADS_PAD, q), lambda b, c: (0, row(b, c))),
                  pl.BlockSpec((None, SSD_CONV, SSD_CONV_DIM), lambda b, c: (layer, 0, 0)),
                  vec(SSD_CONV_DIM), vec(HEADS_PAD), vec_t, vec(HEADS_PAD), vec_t,
                  vec(D_INNER), vec(D_INNER)],
        out_specs=[pl.BlockSpec((q, D_INNER), lambda b, c: (row(b, c), 0)),
                   pl.BlockSpec((None, SSD_HEADS, SSD_HEAD_DIM, D_STATE), lambda b, c: (b, 0, 0, 0))],
        out_shape=[jax.ShapeDtypeStruct((m, D_INNER), BF16),
                   jax.ShapeDtypeStruct((bsz, SSD_HEADS, SSD_HEAD_DIM, D_STATE), F32)],
        scratch_shapes=[pltpu.VMEM((q + 2 * SUBLANES_V7X, SSD_CONV_DIM), F32),
                        pltpu.VMEM((q, SSD_CONV_DIM), F32),
                        pltpu.VMEM((q, SSD_GDIM), F32)],
        compiler_params=_cparams("parallel", "arbitrary"),
        name="ssd_chunk",
    )(proj, proj, proj, proj, proj, dt_raw, dt_raw.T, w["conv_w"], w["conv_b"],
      w["dt_bias"], w["dt_bias_t"], w["a_log"], w["a_log_t"], w["d_skip"], w["g_norm"])
    return y, h


def _conv_time_major(st_ref, x_ref, w_ref, b_ref, nb):
    width = w_ref.shape[0]
    hist = (width - 1) * nb
    rows = x_ref.shape[0]
    acc = b_ref[...]
    for k in range(width):
        parts = []
        if k * nb < hist:
            parts.append(st_ref[k * nb:hist, :])
        take = rows - (hist - k * nb)
        if take > 0:
            parts.append(x_ref[0:take, :])
        piece = parts[0] if len(parts) == 1 else jnp.concatenate(parts, axis=0)
        acc = acc + piece[0:rows] * w_ref[k:k + 1, :]
    return _silu(acc)


def _ssd_sample_prep_kernel(xr_ref, br_ref, cr_ref, z_ref, sx_ref, sb_ref, sc_ref, wx_ref, wb_ref, wc_ref,
                            bx_ref, bb_ref, bcb_ref, dtr_ref, dtb_ref, alog_ref, dskip_ref,
                            y2_ref, ae_ref, xw_ref, bm_ref, cm_ref, zo_ref, dl_ref, *, nb, nt):
    g = pl.program_id(0)
    xs = _conv_time_major(sx_ref, xr_ref, wx_ref, bx_ref, nb)
    bm = _conv_time_major(sb_ref, br_ref, wb_ref, bb_ref, nb)
    cm = _conv_time_major(sc_ref, cr_ref, wc_ref, bcb_ref, nb)
    bmr = bm.astype(BF16).astype(F32)
    cmr = cm.astype(BF16).astype(F32)

    dt = _softplus(dtr_ref[...] + dtb_ref[...])
    da = dt * (-jnp.exp(alog_ref[...]))
    sl = lambda a, t: a[t * nb:(t + 1) * nb]
    acums = [sl(da, 0)]
    for t in range(1, nt):
        acums.append(acums[-1] + sl(da, t))
    dl_ref[...] = jnp.exp(acums[-1])

    row = lax.broadcasted_iota(jnp.int32, (HEADS_PAD, SSD_GDIM), 0)
    lane = lax.broadcasted_iota(jnp.int32, (HEADS_PAD, SSD_GDIM), 1)
    e01 = jnp.where(row == g * SSD_HPG + lane // SSD_HEAD_DIM, 1.0, 0.0).astype(BF16)
    ex = _dot_exact_rhs01(jnp.concatenate(acums + [dt], axis=0), e01)
    a_e = [sl(ex, t) for t in range(nt)]
    d_e = [sl(ex, nt + t) for t in range(nt)]
    for t in range(nt):
        y2 = sl(xs, t) * dskip_ref[...]
        for j in range(t + 1):
            cb = jnp.sum(sl(cmr, t) * sl(bmr, j), axis=-1, keepdims=True)
            y2 = y2 + jnp.exp(a_e[t] - a_e[j]) * d_e[j] * cb * sl(xs, j)
        y2_ref[:, t, :] = y2
        ae_ref[:, t, :] = jnp.exp(a_e[t])
        xw_ref[:, t, :] = jnp.exp(a_e[nt - 1] - a_e[t]) * d_e[t] * sl(xs, t)
        bm_ref[:, t, :] = sl(bm, t)
        cm_ref[:, t, :] = sl(cm, t)
        zo_ref[:, t, :] = z_ref[t * nb:(t + 1) * nb, :]


def _ssd_sample_state_kernel(dl_ref, h0_ref, cm_ref, bm_ref, xw_ref, ae_ref, y2_ref, z_ref, gn_ref, *rest,
                             n_prev):
    if n_prev:
        prev_ref, y_ref, h_ref = rest
        for l in range(n_prev):
            h_ref[l] = prev_ref[l]
    else:
        y_ref, h_ref = rest
    b = pl.program_id(0)
    for g in range(SSD_GROUPS):
        gcols = slice(g * SSD_GDIM, (g + 1) * SSD_GDIM)
        scols = slice(g * D_STATE, (g + 1) * D_STATE)
        hg = h0_ref[g * SSD_HPG:(g + 1) * SSD_HPG].reshape(SSD_HPG * SSD_HEAD_DIM, D_STATE)
        y_off = _dot_nt(cm_ref[:, scols].astype(BF16), hg.astype(BF16))
        s_new = _dot_tn(xw_ref[:, gcols].astype(BF16), bm_ref[:, scols].astype(BF16))
        for r in range(SSD_HPG):
            h = g * SSD_HPG + r
            rows = slice(r * SSD_HEAD_DIM, (r + 1) * SSD_HEAD_DIM)
            h_ref[n_prev, h] = hg[rows] * dl_ref[b, h] + s_new[rows]
        yv = (ae_ref[:, gcols] * y_off + y2_ref[:, gcols]) * _silu(z_ref[:, gcols])
        ms = jnp.mean(yv * yv, axis=-1, keepdims=True)
        y_ref[:, gcols] = yv * lax.rsqrt(ms + EPS) * gn_ref[:, gcols]


def ssd_sample(proj, dt_raw, conv_state, h0_stack, h_prev, nb, layer, w):
    m = proj.shape[0]
    nt = m // nb
    hist = (SSD_CONV - 1) * nb
    xblk = D_INNER // SSD_GDIM
    bblk = 2 * D_INNER // D_STATE
    cblk = bblk + SSD_GROUPS
    sbb = D_INNER // D_STATE
    scb = sbb + SSD_GROUPS
    vec = lambda width: pl.BlockSpec((None, 1, width), lambda g: (layer, 0, 0))
    bm_out = lambda width: jax.ShapeDtypeStruct((nb, nt, width), F32)
    bm_spec = lambda width: pl.BlockSpec((nb, nt, width), lambda g: (0, 0, g))
    y2, a_e, xw, bm, cm, z_bm, dlast = pl.pallas_call(
        functools.partial(_ssd_sample_prep_kernel, nb=nb, nt=nt),
        grid=(SSD_GROUPS,),
        in_specs=[pl.BlockSpec((m, SSD_GDIM), lambda g: (0, xblk + g)),
                  pl.BlockSpec((m, D_STATE), lambda g: (0, bblk + g)),
                  pl.BlockSpec((m, D_STATE), lambda g: (0, cblk + g)),
                  pl.BlockSpec((m, SSD_GDIM), lambda g: (0, g)),
                  pl.BlockSpec((hist, SSD_GDIM), lambda g: (0, g)),
                  pl.BlockSpec((hist, D_STATE), lambda g: (0, sbb + g)),
                  pl.BlockSpec((hist, D_STATE), lambda g: (0, scb + g)),
                  pl.BlockSpec((None, SSD_CONV, SSD_GDIM), lambda g: (layer, 0, g)),
                  pl.BlockSpec((None, SSD_CONV, D_STATE), lambda g: (layer, 0, sbb + g)),
                  pl.BlockSpec((None, SSD_CONV, D_STATE), lambda g: (layer, 0, scb + g)),
                  pl.BlockSpec((None, 1, SSD_GDIM), lambda g: (layer, 0, g)),
                  pl.BlockSpec((None, 1, D_STATE), lambda g: (layer, 0, sbb + g)),
                  pl.BlockSpec((None, 1, D_STATE), lambda g: (layer, 0, scb + g)),
                  pl.BlockSpec((m, HEADS_PAD), lambda g: (0, 0)),
                  vec(HEADS_PAD), vec(HEADS_PAD),
                  pl.BlockSpec((None, 1, SSD_GDIM), lambda g: (layer, 0, g))],
        out_specs=[bm_spec(SSD_GDIM), bm_spec(SSD_GDIM), bm_spec(SSD_GDIM), bm_spec(D_STATE), bm_spec(D_STATE),
                   bm_spec(SSD_GDIM), pl.BlockSpec((nb, HEADS_PAD), lambda g: (0, 0))],
        out_shape=[bm_out(D_INNER), bm_out(D_INNER), bm_out(D_INNER), bm_out(SSD_BC), bm_out(SSD_BC),
                   bm_out(D_INNER), jax.ShapeDtypeStruct((nb, HEADS_PAD), F32)],
        compiler_params=_cparams("arbitrary"),
        name="ssd_sample_prep",
    )(proj, proj, proj, proj, conv_state, conv_state, conv_state, w["conv_w"], w["conv_w"], w["conv_w"],
      w["conv_b"], w["conv_b"], w["conv_b"], dt_raw, w["dt_bias"], w["a_log"], w["d_skip"])

    n_prev = 0 if h_prev is None else h_prev.shape[0]
    state_dims = (SSD_HEADS, SSD_HEAD_DIM, D_STATE)
    per_b = lambda width: pl.BlockSpec((None, nt, width), lambda b: (b, 0, 0))
    in_specs = [pl.BlockSpec(memory_space=pltpu.SMEM),
                pl.BlockSpec((None, None) + state_dims, lambda b: (layer, b, 0, 0, 0)),
                per_b(SSD_BC), per_b(SSD_BC), per_b(D_INNER), per_b(D_INNER), per_b(D_INNER), per_b(D_INNER),
                pl.BlockSpec((None, 1, D_INNER), lambda b: (layer, 0, 0))]
    args = [dlast[:, :SSD_HEADS], h0_stack, cm, bm, xw, a_e, y2, z_bm, w["g_norm"]]
    if n_prev:
        in_specs.append(pl.BlockSpec((n_prev, None) + state_dims, lambda b: (0, b, 0, 0, 0)))
        args.append(h_prev)
    y, h = pl.pallas_call(
        functools.partial(_ssd_sample_state_kernel, n_prev=n_prev),
        grid=(nb,),
        in_specs=in_specs,
        out_specs=[per_b(D_INNER),
                   pl.BlockSpec((n_prev + 1, None) + state_dims, lambda b: (0, b, 0, 0, 0))],
        out_shape=[jax.ShapeDtypeStruct((nb, nt, D_INNER), F32),
                   jax.ShapeDtypeStruct((n_prev + 1, nb) + state_dims, F32)],
        compiler_params=_cparams("parallel"),
        name="ssd_sample_state",
    )(*args)
    y_tm = y.transpose(1, 0, 2).reshape(m, D_INNER).astype(BF16)
    return y_tm, h


def _pool_kernel(*refs, tm, stride, halo, tail, tiles_per_seq, pos0, has_state):
    if has_state:
        u_ref, win_ref, wg_ref, sc_ref, st_ref, mixed_ref, vtail_ref, ext_ref = refs
    else:
        u_ref, win_ref, wg_ref, sc_ref, mixed_ref, vtail_ref, ext_ref, carry_ref = refs
    i = pl.program_id(0)
    g = pl.program_id(1)
    width = ext_ref.shape[1]
    if has_state:
        ext_ref[0:halo, :] = st_ref[...]
    else:
        first = i % tiles_per_seq == 0

        @pl.when(first)
        def _():
            ext_ref[0:halo, :] = jnp.zeros((halo, width), F32)

        @pl.when(jnp.logical_not(first))
        def _():
            ext_ref[0:halo, :] = carry_ref[g]

    v = _dot(u_ref[...], win_ref[...].astype(BF16))
    ext_ref[halo:halo + tm, :] = v
    vtail_ref[...] = v[tm - tail:tm]
    if not has_state:
        carry_ref[g] = v[tm - halo:tm]
    t0 = (i % tiles_per_seq) * (tm // stride)
    t = t0 + lax.broadcasted_iota(jnp.int32, (tm, width), 0) // stride

    for gi, win in enumerate(POOL_WINDOWS):
        @pl.when(g == gi)
        def _(win=win):
            s = v
            for d in range(1, win):
                s = s + ext_ref[pl.ds(halo - d * stride, tm), :]
            cnt = jnp.minimum(pos0 + t + 1, win).astype(F32)
            p = (s / cnt - v).astype(BF16)
            mixed_ref[...] = (_dot(p, wg_ref[...].astype(BF16)) * sc_ref[...]).astype(mixed_ref.dtype)


def pool_mix(u, w_in, w_grp, scale3, layer, *, n_seq, stride, pos0, state=None):
    m, d = u.shape
    ng = len(POOL_WINDOWS)
    gdim = w_in.shape[-1] // ng
    has_state = state is not None
    if has_state:
        tm, halo, tail, tiles_per_seq = m, state.shape[0], m, 1
    else:
        tm, halo, tail = min(m // n_seq, 1024), 2 * SUBLANES_V7X, 2 * SUBLANES_V7X
        tiles_per_seq = (m // n_seq) // tm
    in_specs = [pl.BlockSpec((tm, d), lambda i, g: (i, 0)),
                pl.BlockSpec((None, d, gdim), lambda i, g: (layer, 0, g)),
                pl.BlockSpec((None, None, gdim, gdim), lambda i, g: (layer, g, 0, 0)),
                pl.BlockSpec((None, 1, gdim), lambda i, g: (layer, 0, g))]
    args = [u, w_in, w_grp, scale3]
    scratch = [pltpu.VMEM((halo + tm, gdim), F32)]
    if has_state:
        in_specs.append(pl.BlockSpec((halo, gdim), lambda i, g: (0, g)))
        args.append(state)
    else:
        scratch.append(pltpu.VMEM((ng, halo, gdim), F32))
    return pl.pallas_call(
        functools.partial(_pool_kernel, tm=tm, stride=stride, halo=halo, tail=tail,
                          tiles_per_seq=tiles_per_seq, pos0=pos0, has_state=has_state),
        grid=(m // tm, ng),
        in_specs=in_specs,
        out_specs=[pl.BlockSpec((tm, gdim), lambda i, g: (i, g)),
                   pl.BlockSpec((tail, gdim), lambda i, g: (i, g))],
        out_shape=[jax.ShapeDtypeStruct((m, ng * gdim), BF16),
                   jax.ShapeDtypeStruct((m // tm * tail, ng * gdim), F32)],
        scratch_shapes=scratch,
        compiler_params=_cparams("arbitrary", "arbitrary"),
        name="pool_mix",
    )(*args)


def _attn_kernel(q_ref, k_ref, v_ref, o_ref):
    hd = q_ref.shape[-1] // XA_HEADS
    scale = hd ** -0.5
    for h in range(XA_HEADS):
        cols = slice(h * hd, (h + 1) * hd)
        s = _dot_nt(q_ref[:, cols], k_ref[:, cols].astype(BF16)) * scale
        e = jnp.exp(s - jnp.max(s, axis=-1, keepdims=True))
        p = e * (1.0 / jnp.sum(e, axis=-1, keepdims=True))
        o_ref[:, cols] = _dot(p.astype(BF16), v_ref[:, cols].astype(BF16)).astype(o_ref.dtype)


def attention(q, k, v, kv_layer):
    bsz, n, d = q.shape
    n_mem = k.shape[2]
    tq = min(n, 1024)
    kv_spec = pl.BlockSpec((None, None, n_mem, d), lambda b, i: (kv_layer, b, 0, 0))
    return pl.pallas_call(
        _attn_kernel,
        grid=(bsz, n // tq),
        in_specs=[pl.BlockSpec((None, tq, d), lambda b, i: (b, i, 0)), kv_spec, kv_spec],
        out_specs=pl.BlockSpec((None, tq, d), lambda b, i: (b, i, 0)),
        out_shape=jax.ShapeDtypeStruct((bsz, n, d), BF16),
        compiler_params=_cparams("parallel", "arbitrary"),
        name="mem_attention",
    )(q, k, v)


def _attn_cache_kernel(q_ref, k_ref, v_ref, o_ref, *, tp):
    n_mem, heads, hd = k_ref.shape
    rows = n_mem * heads
    kf = k_ref[...].reshape(rows, hd).astype(BF16)
    vf = v_ref[...].reshape(rows, hd).astype(BF16)
    s_t = _dot_nt(kf, q_ref[...]) * (hd ** -0.5)
    r = lax.broadcasted_iota(jnp.int32, s_t.shape, 0)
    c = lax.broadcasted_iota(jnp.int32, s_t.shape, 1)
    s_t = jnp.where(r % heads == c // tp, s_t, -jnp.inf)
    e = jnp.exp(s_t - jnp.max(s_t, axis=0, keepdims=True))
    p_t = e * (1.0 / jnp.sum(e, axis=0, keepdims=True))
    o_ref[...] = _dot_tn(p_t.astype(BF16), vf).astype(o_ref.dtype)


def attention_cache(q, k, v, kv_layer, tp):
    bsz = q.shape[0]
    kv_spec = pl.BlockSpec((None, None) + k.shape[2:], lambda b: (kv_layer, b, 0, 0, 0))
    q_spec = pl.BlockSpec((None,) + q.shape[1:], lambda b: (b, 0, 0))
    return pl.pallas_call(
        functools.partial(_attn_cache_kernel, tp=tp),
        grid=(bsz,),
        in_specs=[q_spec, kv_spec, kv_spec],
        out_specs=q_spec,
        out_shape=jax.ShapeDtypeStruct(q.shape, BF16),
        compiler_params=_cparams("parallel"),
        name="cache_attention",
    )(q, k, v)


def _ffn_gate_kernel(*refs, tm, stride, halo, tail, tiles_per_seq, has_state):
    if has_state:
        u_ref, wg_ref, wu_ref, cw_ref, cb_ref, st_ref, h_ref, gtail_ref, ext_ref = refs
    else:
        u_ref, wg_ref, wu_ref, cw_ref, cb_ref, h_ref, gtail_ref, ext_ref, carry_ref = refs
    i = pl.program_id(0)
    j = pl.program_id(1)
    u = u_ref[...]
    if has_state:
        ext_ref[0:halo, :] = st_ref[...]
    else:
        first = i % tiles_per_seq == 0

        @pl.when(first)
        def _():
            ext_ref[0:halo, :] = jnp.zeros((halo, ext_ref.shape[1]), F32)

        @pl.when(jnp.logical_not(first))
        def _():
            ext_ref[0:halo, :] = carry_ref[j]

    gate = _dot(u, wg_ref[...].astype(BF16))
    up = _dot(u, wu_ref[...].astype(BF16))
    ext_ref[halo:halo + tm, :] = gate
    acc = cb_ref[...] + gate * cw_ref[FFN_CONV - 1:FFN_CONV, :]
    for k in range(FFN_CONV - 1):
        acc = acc + ext_ref[pl.ds(halo - (FFN_CONV - 1 - k) * stride, tm), :] * cw_ref[k:k + 1, :]
    h_ref[...] = (_silu(acc) * up).astype(h_ref.dtype)
    gtail_ref[...] = ext_ref[halo + tm - tail:halo + tm, :]
    if not has_state:
        carry_ref[j] = ext_ref[tm:tm + halo, :]


def ffn_gate(u, w_gate, w_up, conv_w, conv_b3, layer, *, n_seq, stride, state=None, tn=512):
    m, d = u.shape
    nf = w_gate.shape[-1]
    has_state = state is not None
    if has_state:
        tm, halo, tail, tiles_per_seq = m, state.shape[0], state.shape[0], 1
    else:
        tm, halo, tail = min(m // n_seq, 1024), SUBLANES_V7X, SUBLANES_V7X
        tiles_per_seq = (m // n_seq) // tm
    wspec = pl.BlockSpec((None, d, tn), lambda i, j: (layer, 0, j))
    in_specs = [pl.BlockSpec((tm, d), lambda i, j: (i, 0)), wspec, wspec,
                pl.BlockSpec((None, FFN_CONV, tn), lambda i, j: (layer, 0, j)),
                pl.BlockSpec((None, 1, tn), lambda i, j: (layer, 0, j))]
    args = [u, w_gate, w_up, conv_w, conv_b3]
    scratch = [pltpu.VMEM((halo + tm, tn), F32)]
    if has_state:
        in_specs.append(pl.BlockSpec((halo, tn), lambda i, j: (0, j)))
        args.append(state)
    else:
        scratch.append(pltpu.VMEM((nf // tn, halo, tn), F32))
    return pl.pallas_call(
        functools.partial(_ffn_gate_kernel, tm=tm, stride=stride, halo=halo, tail=tail,
                          tiles_per_seq=tiles_per_seq, has_state=has_state),
        grid=(m // tm, nf // tn),
        in_specs=in_specs,
        out_specs=[pl.BlockSpec((tm, tn), lambda i, j: (i, j)),
                   pl.BlockSpec((tail, tn), lambda i, j: (i, j))],
        out_shape=[jax.ShapeDtypeStruct((m, nf), BF16),
                   jax.ShapeDtypeStruct((m // tm * tail, nf), F32)],
        scratch_shapes=scratch,
        compiler_params=_cparams("arbitrary", "arbitrary"),
        name="ffn_gate",
    )(*args)


def _ssd_weights(p):
    pad = HEADS_PAD - SSD_HEADS
    nl = p["ssd_dt_bias"].shape[0]
    padv = lambda a: jnp.pad(a, ((0, 0), (0, pad)))
    dt_bias = padv(p["ssd_dt_bias"])
    a_log = padv(p["ssd_a_log"])
    return dict(
        w_in_nk=jnp.swapaxes(p["ssd_w_in"], 1, 2),
        w_dt=jnp.pad(p["ssd_w_in"][:, :, D_INNER + SSD_CONV_DIM:], ((0, 0), (0, 0), (0, pad))),
        conv_w=p["ssd_conv_w"],
        conv_b=p["ssd_conv_b"].reshape(nl, 1, SSD_CONV_DIM),
        dt_bias=dt_bias.reshape(nl, 1, HEADS_PAD), dt_bias_t=dt_bias.reshape(nl, HEADS_PAD, 1),
        a_log=a_log.reshape(nl, 1, HEADS_PAD), a_log_t=a_log.reshape(nl, HEADS_PAD, 1),
        d_skip=jnp.repeat(p["ssd_d"], SSD_HEAD_DIM, axis=1).reshape(nl, 1, D_INNER),
        g_norm=p["ssd_norm"].reshape(nl, 1, D_INNER))


def _trunk(x, p, sw, *, prompt, bsz, mem_k, mem_v, state_ssm=None, state_ssm_conv=None, state_pool=None,
           state_ffn_conv=None):
    m, d = x.shape
    n = m // bsz
    depth = p["norm_mix"].shape[0]
    stride = 1 if prompt else bsz
    n_seq = bsz if prompt else 1
    proj_cols = D_INNER + SSD_CONV_DIM
    ssm_out, sconv_out, pool_out, fconv_out = [], [], [], []
    ssm_all = None

    def time_major_state(s):
        return s.transpose(1, 0, 2).reshape(s.shape[1] * s.shape[0], s.shape[2])

    for i in range(depth):
        j = i // 2
        u = rmsnorm(x, p["norm_mix"], i, BF16)
        if i % 2 == 0:
            proj = matmul(u, sw["w_in_nk"], j, ncols=proj_cols, w_is_nk=True)
            dt_raw = matmul(u, sw["w_dt"], j)
            if prompt:
                y, h = ssd_prompt(proj, dt_raw, bsz, j, sw)
                ssm_out.append(h)
                sconv_out.append(proj.reshape(bsz, n, proj_cols)[:, n - (SSD_CONV - 1):, D_INNER:])
            else:
                y, ssm_all = ssd_sample(proj, dt_raw, time_major_state(state_ssm_conv[j]), state_ssm, ssm_all,
                                        bsz, j, sw)
                xbc = proj.reshape(n, bsz, proj_cols)[:, :, D_INNER:]
                ext = jnp.concatenate([state_ssm_conv[j], xbc.transpose(1, 0, 2)], axis=1)
                sconv_out.append(ext[:, -(SSD_CONV - 1):])
            x = matmul(y, p["ssd_w_out"], j, res=x)
        else:
            scale3 = p["pool_scale"].reshape(-1, 1, p["pool_scale"].shape[-1])
            if prompt:
                mixed, vtail = pool_mix(u, p["pool_w_in"], p["pool_w_grp"], scale3, j, n_seq=n_seq, stride=1,
                                        pos0=0)
                nbuf = max(POOL_WINDOWS) - 1
                pool_out.append(vtail.reshape(bsz, -1, vtail.shape[-1])[:, -nbuf:])
            else:
                mixed, vtail = pool_mix(u, p["pool_w_in"], p["pool_w_grp"], scale3, j, n_seq=n_seq, stride=bsz,
                                        pos0=PAST_LEN, state=time_major_state(state_pool[j]))
                v_bm = vtail.reshape(n, bsz, -1).transpose(1, 0, 2)
                nbuf = state_pool.shape[2]
                pool_out.append(jnp.concatenate([state_pool[j], v_bm], axis=1)[:, -nbuf:])
            x = matmul(mixed, p["pool_w_out"], j, res=x)

        u = rmsnorm(x, p["norm_xattn"], i, BF16)
        qm = matmul(u, p["xa_w_q"], i, out_dtype=BF16)
        if prompt:
            o = attention(qm.reshape(bsz, n, d), mem_k, mem_v, i).reshape(m, d)
        else:
            hd = d // XA_HEADS
            q4 = qm.reshape(n, bsz, XA_HEADS, hd).transpose(1, 2, 0, 3).reshape(bsz, XA_HEADS * n, hd)
            o4 = attention_cache(q4, mem_k, mem_v, i, n).reshape(bsz, XA_HEADS, n, hd)
            o = o4.transpose(2, 0, 1, 3).reshape(m, d)
        x = matmul(o, p["xa_w_o"], i, res=x)

        u = rmsnorm(x, p["norm_ffn"], i, BF16)
        cb3 = p["ffn_conv_b"].reshape(depth, 1, -1)
        if prompt:
            hcat, gtail = ffn_gate(u, p["ffn_w_gate"], p["ffn_w_up"], p["ffn_conv_w"], cb3, i, n_seq=n_seq,
                                   stride=1)
            fconv_out.append(gtail.reshape(bsz, -1, gtail.shape[-1])[:, -(FFN_CONV - 1):])
        else:
            hcat, gtail = ffn_gate(u, p["ffn_w_gate"], p["ffn_w_up"], p["ffn_conv_w"], cb3, i, n_seq=n_seq,
                                   stride=bsz, state=time_major_state(state_ffn_conv[i]))
            fconv_out.append(gtail.reshape(FFN_CONV - 1, bsz, -1).transpose(1, 0, 2))
        x = matmul(hcat, p["ffn_w_down"], i, res=x, tn=256)

    y = rmsnorm(x, p["norm_final"].reshape(1, d), 0, F32)
    ssm_new = jnp.stack(ssm_out) if prompt else ssm_all
    return y, ssm_new, jnp.stack(sconv_out), jnp.stack(pool_out), jnp.stack(fconv_out)


def kernel(x_prompt, x_sample, mem_prompt, cache_mem_k, cache_mem_v, state_ssm, state_ssm_conv, state_pool, state_ffn_conv, norm_mix, norm_xattn, norm_ffn, norm_mem, norm_final, ssd_w_in, ssd_conv_w, ssd_conv_b, ssd_dt_bias, ssd_a_log, ssd_d, ssd_norm, ssd_w_out, pool_w_in, pool_w_grp, pool_scale, pool_w_out, xa_w_q, xa_w_k, xa_w_v, xa_w_o, ffn_w_gate, ffn_conv_w, ffn_conv_b, ffn_w_up, ffn_w_down):
    p = dict(norm_mix=norm_mix, norm_xattn=norm_xattn, norm_ffn=norm_ffn, norm_final=norm_final,
             ssd_w_in=ssd_w_in, ssd_conv_w=ssd_conv_w, ssd_conv_b=ssd_conv_b, ssd_dt_bias=ssd_dt_bias,
             ssd_a_log=ssd_a_log, ssd_d=ssd_d, ssd_norm=ssd_norm, ssd_w_out=ssd_w_out,
             pool_w_in=pool_w_in, pool_w_grp=pool_w_grp, pool_scale=pool_scale, pool_w_out=pool_w_out,
             xa_w_q=xa_w_q, xa_w_o=xa_w_o, ffn_w_gate=ffn_w_gate, ffn_conv_w=ffn_conv_w,
             ffn_conv_b=ffn_conv_b, ffn_w_up=ffn_w_up, ffn_w_down=ffn_w_down)
    sw = _ssd_weights(p)
    depth = norm_mix.shape[0]
    bp, n_p, d = x_prompt.shape
    bs, n_s, _ = x_sample.shape
    n_mem = mem_prompt.shape[1]
    hd = d // XA_HEADS

    mem_k_p, mem_v_p = mem_kv(mem_prompt.reshape(bp * n_mem, d), norm_mem, xa_w_k, xa_w_v)
    mem_k_p = mem_k_p.reshape(depth, bp, n_mem, d)
    mem_v_p = mem_v_p.reshape(depth, bp, n_mem, d)

    y_p, ssm_p, sconv_p, pool_p, fconv_p = _trunk(
        x_prompt.reshape(bp * n_p, d), p, sw, prompt=True, bsz=bp, mem_k=mem_k_p, mem_v=mem_v_p)

    xs_tm = x_sample.transpose(1, 0, 2).reshape(n_s * bs, d)
    y_s, ssm_s, sconv_s, pool_s, fconv_s = _trunk(
        xs_tm, p, sw, prompt=False, bsz=bs,
        mem_k=cache_mem_k, mem_v=cache_mem_v,
        state_ssm=state_ssm, state_ssm_conv=state_ssm_conv, state_pool=state_pool,
        state_ffn_conv=state_ffn_conv)
    y_s = y_s.reshape(n_s, bs, d).transpose(1, 0, 2)

    kv_shape = (depth, bp, n_mem, XA_HEADS, hd)
    return (y_p.reshape(bp, n_p, d), y_s, mem_k_p.reshape(kv_shape), mem_v_p.reshape(kv_shape),
            ssm_p, sconv_p, pool_p, fconv_p, ssm_s, sconv_s, pool_s, fconv_s)
```

```python
import functools

import jax
import jax.numpy as jnp
from jax import lax
from jax.experimental import pallas as pl
from jax.experimental.pallas import tpu as pltpu

F32 = jnp.float32
BF16 = jnp.bfloat16

EPS = 1e-6
PAST_LEN = 16384
POOL_WINDOWS = (2, 4, 8, 16)

SSD_HEAD_DIM = 64
SSD_HEADS = 64
SSD_GROUPS = 8
SSD_HPG = SSD_HEADS // SSD_GROUPS
D_STATE = 128
D_INNER = SSD_HEADS * SSD_HEAD_DIM
SSD_GDIM = D_INNER // SSD_GROUPS
SSD_BC = SSD_GROUPS * D_STATE
SSD_CONV_DIM = D_INNER + 2 * SSD_BC
SSD_CONV = 4
SSD_CHUNK = 128
HEADS_PAD = 128

XA_HEADS = 4
FFN_CONV = 3

SUBLANES_V7X = 8
VMEM_LIMIT_V7X = 50 * 1024 * 1024


def _cparams(*sem):
    return pltpu.CompilerParams(dimension_semantics=sem, vmem_limit_bytes=VMEM_LIMIT_V7X)


def _silu(x):
    return x * (0.5 * jnp.tanh(0.5 * x) + 0.5)


def _softplus(x):
    return jnp.maximum(x, 0.0) + jnp.log1p(jnp.exp(-jnp.abs(x)))


def _dot(a, b):
    return jnp.dot(a, b, preferred_element_type=F32)


def _dot_nt(a, b):
    return lax.dot_general(a, b, (((1,), (1,)), ((), ())), preferred_element_type=F32)


def _dot_tn(a, b):
    return lax.dot_general(a, b, (((0,), (0,)), ((), ())), preferred_element_type=F32)


def _split3(x):
    hi = x.astype(BF16)
    r1 = x - hi.astype(F32)
    mid = r1.astype(BF16)
    lo = (r1 - mid.astype(F32)).astype(BF16)
    return hi, mid, lo


def _dot_split_rhs01(parts, e01):
    return _dot(parts[0], e01) + _dot(parts[1], e01) + _dot(parts[2], e01)


def _dot_exact_rhs01(x, e01):
    return _dot_split_rhs01(_split3(x), e01)


def _dot_exact_lhs01(e01, x):
    hi, mid, lo = _split3(x)
    return _dot(e01, hi) + _dot(e01, mid) + _dot(e01, lo)


def _head_expand_matrix(first_head, width):
    row = lax.broadcasted_iota(jnp.int32, (HEADS_PAD, width), 0)
    lane = lax.broadcasted_iota(jnp.int32, (HEADS_PAD, width), 1)
    return jnp.where(row == first_head + lane // SSD_HEAD_DIM, 1.0, 0.0).astype(BF16)


def _rms_kernel(x_ref, g_ref, o_ref):
    x = x_ref[...]
    ms = jnp.mean(x * x, axis=-1, keepdims=True)
    o_ref[...] = (x * lax.rsqrt(ms + EPS) * g_ref[...]).astype(o_ref.dtype)


def rmsnorm(x, g_stack, layer, out_dtype):
    m, d = x.shape
    tm = min(m, 512)
    g3 = g_stack.reshape(-1, 1, d)
    return pl.pallas_call(
        _rms_kernel,
        grid=(m // tm,),
        in_specs=[pl.BlockSpec((tm, d), lambda i: (i, 0)),
                  pl.BlockSpec((None, 1, d), lambda i: (layer, 0, 0))],
        out_specs=pl.BlockSpec((tm, d), lambda i: (i, 0)),
        out_shape=jax.ShapeDtypeStruct((m, d), out_dtype),
        compiler_params=_cparams("parallel"),
        name="rmsnorm",
    )(x, g3)


def _mm_kernel(x_ref, w_ref, *rest, has_res, w_is_nk):
    o_ref = rest[-1]
    w = w_ref[...].astype(BF16)
    acc = _dot_nt(x_ref[...], w) if w_is_nk else _dot(x_ref[...], w)
    if has_res:
        acc = acc + rest[0][...]
    o_ref[...] = acc.astype(o_ref.dtype)


def matmul(x, w_stack, layer, *, ncols=None, res=None, out_dtype=F32, tn=512, w_is_nk=False):
    m, k = x.shape
    n = ncols or w_stack.shape[1 if w_is_nk else 2]
    tm = 2048 if (k <= 2048 and m % 2048 == 0) else min(m, 1024)
    tn = min(tn, n)
    assert m % tm == 0 and n % tn == 0
    w_spec = (pl.BlockSpec((None, tn, k), lambda i, j: (layer, j, 0)) if w_is_nk
              else pl.BlockSpec((None, k, tn), lambda i, j: (layer, 0, j)))
    in_specs = [pl.BlockSpec((tm, k), lambda i, j: (i, 0)), w_spec]
    args = [x, w_stack]
    if res is not None:
        in_specs.append(pl.BlockSpec((tm, tn), lambda i, j: (i, j)))
        args.append(res)
    return pl.pallas_call(
        functools.partial(_mm_kernel, has_res=res is not None, w_is_nk=w_is_nk),
        grid=(m // tm, n // tn),
        in_specs=in_specs,
        out_specs=pl.BlockSpec((tm, tn), lambda i, j: (i, j)),
        out_shape=jax.ShapeDtypeStruct((m, n), out_dtype),
        compiler_params=_cparams("parallel", "arbitrary"),
        name="matmul",
    )(*args)


def _norm_mm_kernel(x_ref, g_ref, w_ref, o_ref, xn_ref):
    @pl.when(pl.program_id(1) == 0)
    def _():
        _rms_kernel(x_ref, g_ref, xn_ref)

    o_ref[...] = _dot(xn_ref[...], w_ref[...].astype(BF16)).astype(o_ref.dtype)


def norm_matmul(x, g_stack, w_stack, layer, *, out_dtype, tn=512):
    m, k = x.shape
    n = w_stack.shape[2]
    tm = min(m, 1024)
    return pl.pallas_call(
        _norm_mm_kernel,
        grid=(m // tm, n // tn),
        in_specs=[pl.BlockSpec((tm, k), lambda i, j: (i, 0)),
                  pl.BlockSpec((None, 1, k), lambda i, j: (layer, 0, 0)),
                  pl.BlockSpec((None, k, tn), lambda i, j: (layer, 0, j))],
        out_specs=pl.BlockSpec((tm, tn), lambda i, j: (i, j)),
        out_shape=jax.ShapeDtypeStruct((m, n), out_dtype),
        scratch_shapes=[pltpu.VMEM((tm, k), BF16)],
        compiler_params=_cparams("parallel", "arbitrary"),
        name="norm_matmul",
    )(x, g_stack.reshape(-1, 1, k), w_stack)


def _mem_kv_kernel(x_ref, g_ref, wk_ref, wv_ref, k_ref, v_ref, xn_ref):
    @pl.when(pl.program_id(1) == 0)
    def _():
        x = x_ref[...]
        ms = jnp.mean(x * x, axis=-1, keepdims=True)
        xn_ref[...] = (x * lax.rsqrt(ms + EPS) * g_ref[...]).astype(BF16)

    xn = xn_ref[...]
    k_ref[...] = _dot(xn, wk_ref[...].astype(BF16))
    v_ref[...] = _dot(xn, wv_ref[...].astype(BF16))


def mem_kv(mem, g_stack, w_k, w_v, tn=256):
    m, d = mem.shape
    nl = w_k.shape[0]
    w_spec = pl.BlockSpec((None, d, tn), lambda l, j: (l, 0, j))
    o_spec = pl.BlockSpec((None, m, tn), lambda l, j: (l, 0, j))
    out = jax.ShapeDtypeStruct((nl, m, d), F32)
    return pl.pallas_call(
        _mem_kv_kernel,
        grid=(nl, d // tn),
        in_specs=[pl.BlockSpec((m, d), lambda l, j: (0, 0)),
                  pl.BlockSpec((None, 1, d), lambda l, j: (l, 0, 0)), w_spec, w_spec],
        out_specs=[o_spec, o_spec],
        out_shape=[out, out],
        scratch_shapes=[pltpu.VMEM((m, d), BF16)],
        compiler_params=_cparams("arbitrary", "arbitrary"),
        name="mem_kv",
    )(mem, g_stack.reshape(nl, 1, d), w_k, w_v)


def _ssd_chunk_kernel(z0_ref, z1_ref, x0_ref, x1_ref, bc_ref, dtr_ref, dtrt_ref, cw_ref, cb_ref,
                      dtb_ref, dtbt_ref, alog_ref, alogt_ref, dskip_ref, gn_ref,
                      y_ref, h_ref, ext_ref, xbc_ref, yg_ref):
    q = x0_ref.shape[0]
    hist = SUBLANES_V7X
    c = pl.program_id(1)

    @pl.when(c == 0)
    def _():
        ext_ref[0:hist, :] = jnp.zeros((hist, SSD_CONV_DIM), F32)
        h_ref[...] = jnp.zeros(h_ref.shape, F32)

    half = D_INNER // 2
    ext_ref[hist:hist + q, 0:half] = x0_ref[...]
    ext_ref[hist:hist + q, half:D_INNER] = x1_ref[...]
    ext_ref[hist:hist + q, D_INNER:SSD_CONV_DIM] = bc_ref[...]

    for s in range(SSD_CONV_DIM // SSD_GDIM):
        cols = slice(s * SSD_GDIM, (s + 1) * SSD_GDIM)
        full = ext_ref[:, cols]
        acc = cb_ref[:, cols] + full[hist:hist + q] * cw_ref[SSD_CONV - 1:SSD_CONV, cols]
        for k in range(SSD_CONV - 1):
            shifted = pltpu.roll(full, SSD_CONV - 1 - k, axis=0)
            acc = acc + shifted[hist:hist + q] * cw_ref[k:k + 1, cols]
        xbc_ref[:, cols] = _silu(acc)
    ext_ref[0:hist, :] = ext_ref[q:q + hist, :]

    dt = _softplus(dtr_ref[...] + dtb_ref[...])
    da = dt * (-jnp.exp(alog_ref[...]))
    dt_t = _softplus(dtrt_ref[...] + dtbt_ref[...])
    da_t = dt_t * (-jnp.exp(alogt_ref[...]))
    ii = lax.broadcasted_iota(jnp.int32, (q, q), 0)
    jj = lax.broadcasted_iota(jnp.int32, (q, q), 1)
    causal = ii >= jj
    tri = jnp.where(causal, 1.0, 0.0).astype(BF16)
    tri_t = jnp.where(jj >= ii, 1.0, 0.0).astype(BF16)
    acum = _dot_exact_lhs01(tri, da)
    acum_t = _dot_exact_rhs01(da_t, tri_t)
    a_last = acum[q - 1:q, :]
    stack = _split3(jnp.concatenate([dt, jnp.exp(acum), jnp.exp(a_last - acum) * dt], axis=0))
    lane = lax.broadcasted_iota(jnp.int32, (q, 2 * SSD_HEAD_DIM), 1)
    first_of_pair = lane < SSD_HEAD_DIM

    for g in range(SSD_GROUPS):
        gcols = slice(g * SSD_GDIM, (g + 1) * SSD_GDIM)
        ex = _dot_split_rhs01(stack, _head_expand_matrix(g * SSD_HPG, SSD_GDIM))
        dt_e, a_e, wl_e = ex[0:q], ex[q:2 * q], ex[2 * q:3 * q]
        xs = xbc_ref[:, gcols]
        bm = xbc_ref[:, D_INNER + g * D_STATE:D_INNER + (g + 1) * D_STATE].astype(BF16)
        cm = xbc_ref[:, D_INNER + SSD_BC + g * D_STATE:D_INNER + SSD_BC + (g + 1) * D_STATE].astype(BF16)
        xdt = (xs * dt_e).astype(BF16)
        xw = (xs * wl_e).astype(BF16)
        cbm = _dot_nt(cm, bm)
        hg = h_ref[g * SSD_HPG:(g + 1) * SSD_HPG].reshape(SSD_HPG * SSD_HEAD_DIM, D_STATE)
        y_off = _dot_nt(cm, hg.astype(BF16))
        s_new = _dot_tn(xw, bm)
        for pr in range(SSD_HPG // 2):
            pcols = slice(pr * 2 * SSD_HEAD_DIM, (pr + 1) * 2 * SSD_HEAD_DIM)
            ys = []
            for hh in range(2):
                h = g * SSD_HPG + pr * 2 + hh
                diff = acum[:, h:h + 1] - acum_t[h:h + 1, :]
                decay = jnp.exp(jnp.where(causal, diff, -jnp.inf))
                ys.append(_dot((cbm * decay).astype(BF16), xdt[:, pcols]))
            yg_ref[:, pcols] = jnp.where(first_of_pair, ys[0], ys[1]) + a_e[:, pcols] * y_off[:, pcols]
        for r in range(SSD_HPG):
            h = g * SSD_HPG + r
            rows = slice(r * SSD_HEAD_DIM, (r + 1) * SSD_HEAD_DIM)
            h_ref[h] = hg[rows] * jnp.exp(acum_t[h:h + 1, q - 1:q]) + s_new[rows]
        z_ref = z0_ref if g < SSD_GROUPS // 2 else z1_ref
        zc = slice((g % (SSD_GROUPS // 2)) * SSD_GDIM, (g % (SSD_GROUPS // 2) + 1) * SSD_GDIM)
        yv = (yg_ref[...] + xs * dskip_ref[:, gcols]) * _silu(z_ref[:, zc])
        ms = jnp.mean(yv * yv, axis=-1, keepdims=True)
        y_ref[:, gcols] = (yv * lax.rsqrt(ms + EPS) * gn_ref[:, gcols]).astype(y_ref.dtype)


def ssd_prompt(proj, dt_raw, bsz, layer, w):
    m = proj.shape[0]
    n = m // bsz
    q = SSD_CHUNK
    nc = n // q
    half = D_INNER // 2
    row = lambda b, c: b * nc + c
    pspec = lambda cb: pl.BlockSpec((q, half), lambda b, c: (row(b, c), cb))
    vec = lambda width: pl.BlockSpec((None, 1, width), lambda b, c: (layer, 0, 0))
    vec_t = pl.BlockSpec((None, HEADS_PAD, 1), lambda b, c: (layer, 0, 0))
    y, h = pl.pallas_call(
        _ssd_chunk_kernel,
        grid=(bsz, nc),
        in_specs=[pspec(0), pspec(1), pspec(2), pspec(3), pspec(4),
                  pl.BlockSpec((q, HEADS_PAD), lambda b, c: (row(b, c), 0)),
                  pl.BlockSpec((HEADS_PAD, q), lambda b, c: (0, row(b, c))),
                  pl.BlockSpec((None, SSD_CONV, SSD_CONV_DIM), lambda b, c: (layer, 0, 0)),
                  vec(SSD_CONV_DIM), vec(HEADS_PAD), vec_t, vec(HEADS_PAD), vec_t,
                  vec(D_INNER), vec(D_INNER)],
        out_specs=[pl.BlockSpec((q, D_INNER), lambda b, c: (row(b, c), 0)),
                   pl.BlockSpec((None, SSD_HEADS, SSD_HEAD_DIM, D_STATE), lambda b, c: (b, 0, 0, 0))],
        out_shape=[jax.ShapeDtypeStruct((m, D_INNER), BF16),
                   jax.ShapeDtypeStruct((bsz, SSD_HEADS, SSD_HEAD_DIM, D_STATE), F32)],
        scratch_shapes=[pltpu.VMEM((q + 2 * SUBLANES_V7X, SSD_CONV_DIM), F32),
                        pltpu.VMEM((q, SSD_CONV_DIM), F32),
                        pltpu.VMEM((q, SSD_GDIM), F32)],
        compiler_params=_cparams("parallel", "arbitrary"),
        name="ssd_chunk",
    )(proj, proj, proj, proj, proj, dt_raw, dt_raw.T, w["conv_w"], w["conv_b"],
      w["dt_bias"], w["dt_bias_t"], w["a_log"], w["a_log_t"], w["d_skip"], w["g_norm"])
    return y, h


def _conv_time_major(st_ref, x_ref, w_ref, b_ref, nb):
    width = w_ref.shape[0]
    hist = (width - 1) * nb
    rows = x_ref.shape[0]
    acc = b_ref[...]
    for k in range(width):
        parts = []
        if k * nb < hist:
            parts.append(st_ref[k * nb:hist, :])
        take = rows - (hist - k * nb)
        if take > 0:
            parts.append(x_ref[0:take, :])
        piece = parts[0] if len(parts) == 1 else jnp.concatenate(parts, axis=0)
        acc = acc + piece[0:rows] * w_ref[k:k + 1, :]
    return _silu(acc)


def _ssd_sample_prep_kernel(xr_ref, br_ref, cr_ref, z_ref, sx_ref, sb_ref, sc_ref, wx_ref, wb_ref, wc_ref,
                            bx_ref, bb_ref, bcb_ref, dtr_ref, dtb_ref, alog_ref, dskip_ref,
                            y2_ref, ae_ref, xw_ref, bm_ref, cm_ref, zo_ref, dl_ref, *, nb, nt):
    g = pl.program_id(0)
    xs = _conv_time_major(sx_ref, xr_ref, wx_ref, bx_ref, nb)
    bm = _conv_time_major(sb_ref, br_ref, wb_ref, bb_ref, nb)
    cm = _conv_time_major(sc_ref, cr_ref, wc_ref, bcb_ref, nb)
    bmr = bm.astype(BF16).astype(F32)
    cmr = cm.astype(BF16).astype(F32)

    dt = _softplus(dtr_ref[...] + dtb_ref[...])
    da = dt * (-jnp.exp(alog_ref[...]))
    sl = lambda a, t: a[t * nb:(t + 1) * nb]
    acums = [sl(da, 0)]
    for t in range(1, nt):
        acums.append(acums[-1] + sl(da, t))
    dl_ref[...] = jnp.exp(acums[-1])

    row = lax.broadcasted_iota(jnp.int32, (HEADS_PAD, SSD_GDIM), 0)
    lane = lax.broadcasted_iota(jnp.int32, (HEADS_PAD, SSD_GDIM), 1)
    e01 = jnp.where(row == g * SSD_HPG + lane // SSD_HEAD_DIM, 1.0, 0.0).astype(BF16)
    ex = _dot_exact_rhs01(jnp.concatenate(acums + [dt], axis=0), e01)
    a_e = [sl(ex, t) for t in range(nt)]
    d_e = [sl(ex, nt + t) for t in range(nt)]
    for t in range(nt):
        y2 = sl(xs, t) * dskip_ref[...]
        for j in range(t + 1):
            cb = jnp.sum(sl(cmr, t) * sl(bmr, j), axis=-1, keepdims=True)
            y2 = y2 + jnp.exp(a_e[t] - a_e[j]) * d_e[j] * cb * sl(xs, j)
        y2_ref[:, t, :] = y2
        ae_ref[:, t, :] = jnp.exp(a_e[t])
        xw_ref[:, t, :] = jnp.exp(a_e[nt - 1] - a_e[t]) * d_e[t] * sl(xs, t)
        bm_ref[:, t, :] = sl(bm, t)
        cm_ref[:, t, :] = sl(cm, t)
        zo_ref[:, t, :] = z_ref[t * nb:(t + 1) * nb, :]


def _ssd_sample_state_kernel(dl_ref, h0_ref, cm_ref, bm_ref, xw_ref, ae_ref, y2_ref, z_ref, gn_ref, *rest,
                             n_prev):
    if n_prev:
        prev_ref, y_ref, h_ref = rest
        for l in range(n_prev):
            h_ref[l] = prev_ref[l]
    else:
        y_ref, h_ref = rest
    nbb = h0_ref.shape[0]
    for bb in range(nbb):
        b = pl.program_id(0) * nbb + bb
        for g in range(SSD_GROUPS):
            gcols = slice(g * SSD_GDIM, (g + 1) * SSD_GDIM)
            scols = slice(g * D_STATE, (g + 1) * D_STATE)
            hg = h0_ref[bb, g * SSD_HPG:(g + 1) * SSD_HPG].reshape(SSD_HPG * SSD_HEAD_DIM, D_STATE)
            y_off = _dot_nt(cm_ref[bb, :, scols].astype(BF16), hg.astype(BF16))
            s_new = _dot_tn(xw_ref[bb, :, gcols].astype(BF16), bm_ref[bb, :, scols].astype(BF16))
            for r in range(SSD_HPG):
                h = g * SSD_HPG + r
                rows = slice(r * SSD_HEAD_DIM, (r + 1) * SSD_HEAD_DIM)
                h_ref[n_prev, bb, h] = hg[rows] * dl_ref[b, h] + s_new[rows]
            yv = (ae_ref[bb, :, gcols] * y_off + y2_ref[bb, :, gcols]) * _silu(z_ref[bb, :, gcols])
            ms = jnp.mean(yv * yv, axis=-1, keepdims=True)
            y_ref[bb, :, gcols] = yv * lax.rsqrt(ms + EPS) * gn_ref[:, gcols]


def ssd_sample(proj, dt_raw, conv_state, h0_stack, h_prev, nb, layer, w):
    m = proj.shape[0]
    nt = m // nb
    hist = (SSD_CONV - 1) * nb
    xblk = D_INNER // SSD_GDIM
    bblk = 2 * D_INNER // D_STATE
    cblk = bblk + SSD_GROUPS
    sbb = D_INNER // D_STATE
    scb = sbb + SSD_GROUPS
    vec = lambda width: pl.BlockSpec((None, 1, width), lambda g: (layer, 0, 0))
    bm_out = lambda width: jax.ShapeDtypeStruct((nb, nt, width), F32)
    bm_spec = lambda width: pl.BlockSpec((nb, nt, width), lambda g: (0, 0, g))
    y2, a_e, xw, bm, cm, z_bm, dlast = pl.pallas_call(
        functools.partial(_ssd_sample_prep_kernel, nb=nb, nt=nt),
        grid=(SSD_GROUPS,),
        in_specs=[pl.BlockSpec((m, SSD_GDIM), lambda g: (0, xblk + g)),
                  pl.BlockSpec((m, D_STATE), lambda g: (0, bblk + g)),
                  pl.BlockSpec((m, D_STATE), lambda g: (0, cblk + g)),
                  pl.BlockSpec((m, SSD_GDIM), lambda g: (0, g)),
                  pl.BlockSpec((hist, SSD_GDIM), lambda g: (0, g)),
                  pl.BlockSpec((hist, D_STATE), lambda g: (0, sbb + g)),
                  pl.BlockSpec((hist, D_STATE), lambda g: (0, scb + g)),
                  pl.BlockSpec((None, SSD_CONV, SSD_GDIM), lambda g: (layer, 0, g)),
                  pl.BlockSpec((None, SSD_CONV, D_STATE), lambda g: (layer, 0, sbb + g)),
                  pl.BlockSpec((None, SSD_CONV, D_STATE), lambda g: (layer, 0, scb + g)),
                  pl.BlockSpec((None, 1, SSD_GDIM), lambda g: (layer, 0, g)),
                  pl.BlockSpec((None, 1, D_STATE), lambda g: (layer, 0, sbb + g)),
                  pl.BlockSpec((None, 1, D_STATE), lambda g: (layer, 0, scb + g)),
                  pl.BlockSpec((m, HEADS_PAD), lambda g: (0, 0)),
                  vec(HEADS_PAD), vec(HEADS_PAD),
                  pl.BlockSpec((None, 1, SSD_GDIM), lambda g: (layer, 0, g))],
        out_specs=[bm_spec(SSD_GDIM), bm_spec(SSD_GDIM), bm_spec(SSD_GDIM), bm_spec(D_STATE), bm_spec(D_STATE),
                   bm_spec(SSD_GDIM), pl.BlockSpec((nb, HEADS_PAD), lambda g: (0, 0))],
        out_shape=[bm_out(D_INNER), bm_out(D_INNER), bm_out(D_INNER), bm_out(SSD_BC), bm_out(SSD_BC),
                   bm_out(D_INNER), jax.ShapeDtypeStruct((nb, HEADS_PAD), F32)],
        compiler_params=_cparams("arbitrary"),
        name="ssd_sample_prep",
    )(proj, proj, proj, proj, conv_state, conv_state, conv_state, w["conv_w"], w["conv_w"], w["conv_w"],
      w["conv_b"], w["conv_b"], w["conv_b"], dt_raw, w["dt_bias"], w["a_log"], w["d_skip"])

    n_prev = 0 if h_prev is None else h_prev.shape[0]
    state_dims = (SSD_HEADS, SSD_HEAD_DIM, D_STATE)
    nbb = 2 if nb % 2 == 0 else 1
    per_b = lambda width: pl.BlockSpec((nbb, nt, width), lambda b: (b, 0, 0))
    in_specs = [pl.BlockSpec(memory_space=pltpu.SMEM),
                pl.BlockSpec((None, nbb) + state_dims, lambda b: (layer, b, 0, 0, 0)),
                per_b(SSD_BC), per_b(SSD_BC), per_b(D_INNER), per_b(D_INNER), per_b(D_INNER), per_b(D_INNER),
                pl.BlockSpec((None, 1, D_INNER), lambda b: (layer, 0, 0))]
    args = [dlast[:, :SSD_HEADS], h0_stack, cm, bm, xw, a_e, y2, z_bm, w["g_norm"]]
    if n_prev:
        in_specs.append(pl.BlockSpec((n_prev, nbb) + state_dims, lambda b: (0, b, 0, 0, 0)))
        args.append(h_prev)
    y, h = pl.pallas_call(
        functools.partial(_ssd_sample_state_kernel, n_prev=n_prev),
        grid=(nb // nbb,),
        in_specs=in_specs,
        out_specs=[per_b(D_INNER),
                   pl.BlockSpec((n_prev + 1, nbb) + state_dims, lambda b: (0, b, 0, 0, 0))],
        out_shape=[jax.ShapeDtypeStruct((nb, nt, D_INNER), F32),
                   jax.ShapeDtypeStruct((n_prev + 1, nb) + state_dims, F32)],
        compiler_params=_cparams("parallel"),
        name="ssd_sample_state",
    )(*args)
    y_tm = y.transpose(1, 0, 2).reshape(m, D_INNER).astype(BF16)
    return y_tm, h


def _pool_kernel(*refs, tm, stride, halo, tail, tiles_per_seq, pos0, has_state):
    if has_state:
        x_ref, gn_ref, win_ref, wg_ref, sc_ref, st_ref, mixed_ref, vtail_ref, u_ref, ext_ref = refs
    else:
        x_ref, gn_ref, win_ref, wg_ref, sc_ref, mixed_ref, vtail_ref, u_ref, ext_ref, carry_ref = refs
    i = pl.program_id(0)
    g = pl.program_id(1)
    width = ext_ref.shape[1]

    @pl.when(g == 0)
    def _():
        _rms_kernel(x_ref, gn_ref, u_ref)

    if has_state:
        ext_ref[0:halo, :] = st_ref[...]
    else:
        first = i % tiles_per_seq == 0

        @pl.when(first)
        def _():
            ext_ref[0:halo, :] = jnp.zeros((halo, width), F32)

        @pl.when(jnp.logical_not(first))
        def _():
            ext_ref[0:halo, :] = carry_ref[g]

    v = _dot(u_ref[...], win_ref[...].astype(BF16))
    ext_ref[halo:halo + tm, :] = v
    vtail_ref[...] = v[tm - tail:tm]
    if not has_state:
        carry_ref[g] = v[tm - halo:tm]
    t0 = (i % tiles_per_seq) * (tm // stride)
    t = t0 + lax.broadcasted_iota(jnp.int32, (tm, width), 0) // stride

    for gi, win in enumerate(POOL_WINDOWS):
        @pl.when(g == gi)
        def _(win=win):
            s = v
            for d in range(1, win):
                s = s + ext_ref[pl.ds(halo - d * stride, tm), :]
            cnt = jnp.minimum(pos0 + t + 1, win).astype(F32)
            p = (s / cnt - v).astype(BF16)
            mixed_ref[...] = (_dot(p, wg_ref[...].astype(BF16)) * sc_ref[...]).astype(mixed_ref.dtype)


def pool_mix(x, g_stack, g_layer, w_in, w_grp, scale3, layer, *, n_seq, stride, pos0, state=None):
    m, d = x.shape
    ng = len(POOL_WINDOWS)
    gdim = w_in.shape[-1] // ng
    has_state = state is not None
    if has_state:
        tm, halo, tail, tiles_per_seq = m, state.shape[0], m, 1
    else:
        tm, halo, tail = min(m // n_seq, 1024), 2 * SUBLANES_V7X, 2 * SUBLANES_V7X
        tiles_per_seq = (m // n_seq) // tm
    in_specs = [pl.BlockSpec((tm, d), lambda i, g: (i, 0)),
                pl.BlockSpec((None, 1, d), lambda i, g: (g_layer, 0, 0)),
                pl.BlockSpec((None, d, gdim), lambda i, g: (layer, 0, g)),
                pl.BlockSpec((None, None, gdim, gdim), lambda i, g: (layer, g, 0, 0)),
                pl.BlockSpec((None, 1, gdim), lambda i, g: (layer, 0, g))]
    args = [x, g_stack.reshape(-1, 1, d), w_in, w_grp, scale3]
    scratch = [pltpu.VMEM((tm, d), BF16), pltpu.VMEM((halo + tm, gdim), F32)]
    if has_state:
        in_specs.append(pl.BlockSpec((halo, gdim), lambda i, g: (0, g)))
        args.append(state)
    else:
        scratch.append(pltpu.VMEM((ng, halo, gdim), F32))
    return pl.pallas_call(
        functools.partial(_pool_kernel, tm=tm, stride=stride, halo=halo, tail=tail,
                          tiles_per_seq=tiles_per_seq, pos0=pos0, has_state=has_state),
        grid=(m // tm, ng),
        in_specs=in_specs,
        out_specs=[pl.BlockSpec((tm, gdim), lambda i, g: (i, g)),
                   pl.BlockSpec((tail, gdim), lambda i, g: (i, g))],
        out_shape=[jax.ShapeDtypeStruct((m, ng * gdim), BF16),
                   jax.ShapeDtypeStruct((m // tm * tail, ng * gdim), F32)],
        scratch_shapes=scratch,
        compiler_params=_cparams("arbitrary", "arbitrary"),
        name="pool_mix",
    )(*args)


def _attn_kernel(q_ref, k_ref, v_ref, o_ref):
    hd = q_ref.shape[-1] // XA_HEADS
    scale = hd ** -0.5
    for h in range(XA_HEADS):
        cols = slice(h * hd, (h + 1) * hd)
        s = _dot_nt(q_ref[:, cols], k_ref[:, cols].astype(BF16)) * scale
        e = jnp.exp(s - jnp.max(s, axis=-1, keepdims=True))
        p = e * (1.0 / jnp.sum(e, axis=-1, keepdims=True))
        o_ref[:, cols] = _dot(p.astype(BF16), v_ref[:, cols].astype(BF16)).astype(o_ref.dtype)


def attention(q, k, v, kv_layer):
    bsz, n, d = q.shape
    n_mem = k.shape[2]
    tq = min(n, 1024)
    kv_spec = pl.BlockSpec((None, None, n_mem, d), lambda b, i: (kv_layer, b, 0, 0))
    return pl.pallas_call(
        _attn_kernel,
        grid=(bsz, n // tq),
        in_specs=[pl.BlockSpec((None, tq, d), lambda b, i: (b, i, 0)), kv_spec, kv_spec],
        out_specs=pl.BlockSpec((None, tq, d), lambda b, i: (b, i, 0)),
        out_shape=jax.ShapeDtypeStruct((bsz, n, d), BF16),
        compiler_params=_cparams("parallel", "arbitrary"),
        name="mem_attention",
    )(q, k, v)


def _attn_cache_kernel(q_ref, k_ref, v_ref, o_ref, *, tp):
    nbb, n_mem, heads, hd = k_ref.shape
    rows = n_mem * heads
    for bb in range(nbb):
        kf = k_ref[bb].reshape(rows, hd).astype(BF16)
        vf = v_ref[bb].reshape(rows, hd).astype(BF16)
        s_t = _dot_nt(kf, q_ref[bb]) * (hd ** -0.5)
        r = lax.broadcasted_iota(jnp.int32, s_t.shape, 0)
        c = lax.broadcasted_iota(jnp.int32, s_t.shape, 1)
        s_t = jnp.where(r % heads == c // tp, s_t, -jnp.inf)
        e = jnp.exp(s_t - jnp.max(s_t, axis=0, keepdims=True))
        p_t = e * (1.0 / jnp.sum(e, axis=0, keepdims=True))
        o_ref[bb] = _dot_tn(p_t.astype(BF16), vf).astype(o_ref.dtype)


def attention_cache(q, k, v, kv_layer, tp):
    bsz = q.shape[0]
    nbb = 2 if bsz % 2 == 0 else 1
    kv_spec = pl.BlockSpec((None, nbb) + k.shape[2:], lambda b: (kv_layer, b, 0, 0, 0))
    q_spec = pl.BlockSpec((nbb,) + q.shape[1:], lambda b: (b, 0, 0))
    return pl.pallas_call(
        functools.partial(_attn_cache_kernel, tp=tp),
        grid=(bsz // nbb,),
        in_specs=[q_spec, kv_spec, kv_spec],
        out_specs=q_spec,
        out_shape=jax.ShapeDtypeStruct(q.shape, BF16),
        compiler_params=_cparams("parallel"),
        name="cache_attention",
    )(q, k, v)


def _ffn_gate_kernel(*refs, tm, stride, halo, tail, tiles_per_seq, has_state):
    if has_state:
        u_ref, wg_ref, wu_ref, cw_ref, cb_ref, st_ref, h_ref, gtail_ref, ext_ref = refs
    else:
        u_ref, wg_ref, wu_ref, cw_ref, cb_ref, h_ref, gtail_ref, ext_ref, carry_ref = refs
    i = pl.program_id(0)
    j = pl.program_id(1)
    u = u_ref[...]
    if has_state:
        ext_ref[0:halo, :] = st_ref[...]
    else:
        first = i % tiles_per_seq == 0

        @pl.when(first)
        def _():
            ext_ref[0:halo, :] = jnp.zeros((halo, ext_ref.shape[1]), F32)

        @pl.when(jnp.logical_not(first))
        def _():
            ext_ref[0:halo, :] = carry_ref[j]

    gate = _dot(u, wg_ref[...].astype(BF16))
    up = _dot(u, wu_ref[...].astype(BF16))
    ext_ref[halo:halo + tm, :] = gate
    acc = cb_ref[...] + gate * cw_ref[FFN_CONV - 1:FFN_CONV, :]
    for k in range(FFN_CONV - 1):
        acc = acc + ext_ref[pl.ds(halo - (FFN_CONV - 1 - k) * stride, tm), :] * cw_ref[k:k + 1, :]
    h_ref[...] = (_silu(acc) * up).astype(h_ref.dtype)
    gtail_ref[...] = ext_ref[halo + tm - tail:halo + tm, :]
    if not has_state:
        carry_ref[j] = ext_ref[tm:tm + halo, :]


def ffn_gate(u, w_gate, w_up, conv_w, conv_b3, layer, *, n_seq, stride, state=None, tn=512):
    m, d = u.shape
    nf = w_gate.shape[-1]
    has_state = state is not None
    if has_state:
        tm, halo, tail, tiles_per_seq = m, state.shape[0], state.shape[0], 1
    else:
        tm, halo, tail = min(m // n_seq, 1024), SUBLANES_V7X, SUBLANES_V7X
        tiles_per_seq = (m // n_seq) // tm
    wspec = pl.BlockSpec((None, d, tn), lambda i, j: (layer, 0, j))
    in_specs = [pl.BlockSpec((tm, d), lambda i, j: (i, 0)), wspec, wspec,
                pl.BlockSpec((None, FFN_CONV, tn), lambda i, j: (layer, 0, j)),
                pl.BlockSpec((None, 1, tn), lambda i, j: (layer, 0, j))]
    args = [u, w_gate, w_up, conv_w, conv_b3]
    scratch = [pltpu.VMEM((halo + tm, tn), F32)]
    if has_state:
        in_specs.append(pl.BlockSpec((halo, tn), lambda i, j: (0, j)))
        args.append(state)
    else:
        scratch.append(pltpu.VMEM((nf // tn, halo, tn), F32))
    return pl.pallas_call(
        functools.partial(_ffn_gate_kernel, tm=tm, stride=stride, halo=halo, tail=tail,
                          tiles_per_seq=tiles_per_seq, has_state=has_state),
        grid=(m // tm, nf // tn),
        in_specs=in_specs,
        out_specs=[pl.BlockSpec((tm, tn), lambda i, j: (i, j)),
                   pl.BlockSpec((tail, tn), lambda i, j: (i, j))],
        out_shape=[jax.ShapeDtypeStruct((m, nf), BF16),
                   jax.ShapeDtypeStruct((m // tm * tail, nf), F32)],
        scratch_shapes=scratch,
        compiler_params=_cparams("arbitrary", "arbitrary"),
        name="ffn_gate",
    )(*args)


def _ssd_weights(p):
    pad = HEADS_PAD - SSD_HEADS
    nl = p["ssd_dt_bias"].shape[0]
    padv = lambda a: jnp.pad(a, ((0, 0), (0, pad)))
    dt_bias = padv(p["ssd_dt_bias"])
    a_log = padv(p["ssd_a_log"])
    return dict(
        w_in_nk=jnp.swapaxes(p["ssd_w_in"], 1, 2),
        w_dt=jnp.pad(p["ssd_w_in"][:, :, D_INNER + SSD_CONV_DIM:], ((0, 0), (0, 0), (0, pad))),
        conv_w=p["ssd_conv_w"],
        conv_b=p["ssd_conv_b"].reshape(nl, 1, SSD_CONV_DIM),
        dt_bias=dt_bias.reshape(nl, 1, HEADS_PAD), dt_bias_t=dt_bias.reshape(nl, HEADS_PAD, 1),
        a_log=a_log.reshape(nl, 1, HEADS_PAD), a_log_t=a_log.reshape(nl, HEADS_PAD, 1),
        d_skip=jnp.repeat(p["ssd_d"], SSD_HEAD_DIM, axis=1).reshape(nl, 1, D_INNER),
        g_norm=p["ssd_norm"].reshape(nl, 1, D_INNER))


def _trunk(x, p, sw, *, prompt, bsz, mem_k, mem_v, state_ssm=None, state_ssm_conv=None, state_pool=None,
           state_ffn_conv=None):
    m, d = x.shape
    n = m // bsz
    depth = p["norm_mix"].shape[0]
    stride = 1 if prompt else bsz
    n_seq = bsz if prompt else 1
    proj_cols = D_INNER + SSD_CONV_DIM
    ssm_out, sconv_out, pool_out, fconv_out = [], [], [], []
    ssm_all = None

    def time_major_state(s):
        return s.transpose(1, 0, 2).reshape(s.shape[1] * s.shape[0], s.shape[2])

    for i in range(depth):
        j = i // 2
        if i % 2 == 0:
            u = rmsnorm(x, p["norm_mix"], i, BF16)
            proj = matmul(u, sw["w_in_nk"], j, ncols=proj_cols, w_is_nk=True)
            dt_raw = matmul(u, sw["w_dt"], j)
            if prompt:
                y, h = ssd_prompt(proj, dt_raw, bsz, j, sw)
                ssm_out.append(h)
                sconv_out.append(proj.reshape(bsz, n, proj_cols)[:, n - (SSD_CONV - 1):, D_INNER:])
            else:
                y, ssm_all = ssd_sample(proj, dt_raw, time_major_state(state_ssm_conv[j]), state_ssm, ssm_all,
                                        bsz, j, sw)
                xbc = proj.reshape(n, bsz, proj_cols)[:, :, D_INNER:]
                ext = jnp.concatenate([state_ssm_conv[j], xbc.transpose(1, 0, 2)], axis=1)
                sconv_out.append(ext[:, -(SSD_CONV - 1):])
            x = matmul(y, p["ssd_w_out"], j, res=x)
        else:
            scale3 = p["pool_scale"].reshape(-1, 1, p["pool_scale"].shape[-1])
            if prompt:
                mixed, vtail = pool_mix(x, p["norm_mix"], i, p["pool_w_in"], p["pool_w_grp"], scale3, j,
                                        n_seq=n_seq, stride=1, pos0=0)
                nbuf = max(POOL_WINDOWS) - 1
                pool_out.append(vtail.reshape(bsz, -1, vtail.shape[-1])[:, -nbuf:])
            else:
                mixed, vtail = pool_mix(x, p["norm_mix"], i, p["pool_w_in"], p["pool_w_grp"], scale3, j,
                                        n_seq=n_seq, stride=bsz, pos0=PAST_LEN,
                                        state=time_major_state(state_pool[j]))
                v_bm = vtail.reshape(n, bsz, -1).transpose(1, 0, 2)
                nbuf = state_pool.shape[2]
                pool_out.append(jnp.concatenate([state_pool[j], v_bm], axis=1)[:, -nbuf:])
            x = matmul(mixed, p["pool_w_out"], j, res=x)

        qm = norm_matmul(x, p["norm_xattn"], p["xa_w_q"], i, out_dtype=BF16)
        if prompt:
            o = attention(qm.reshape(bsz, n, d), mem_k, mem_v, i).reshape(m, d)
        else:
            hd = d // XA_HEADS
            q4 = qm.reshape(n, bsz, XA_HEADS, hd).transpose(1, 2, 0, 3).reshape(bsz, XA_HEADS * n, hd)
            o4 = attention_cache(q4, mem_k, mem_v, i, n).reshape(bsz, XA_HEADS, n, hd)
            o = o4.transpose(2, 0, 1, 3).reshape(m, d)
        x = matmul(o, p["xa_w_o"], i, res=x)

        u = rmsnorm(x, p["norm_ffn"], i, BF16)
        cb3 = p["ffn_conv_b"].reshape(depth, 1, -1)
        if prompt:
            hcat, gtail = ffn_gate(u, p["ffn_w_gate"], p["ffn_w_up"], p["ffn_conv_w"], cb3, i, n_seq=n_seq,
                                   stride=1)
            fconv_out.append(gtail.reshape(bsz, -1, gtail.shape[-1])[:, -(FFN_CONV - 1):])
        else:
            hcat, gtail = ffn_gate(u, p["ffn_w_gate"], p["ffn_w_up"], p["ffn_conv_w"], cb3, i, n_seq=n_seq,
                                   stride=bsz, state=time_major_state(state_ffn_conv[i]))
            fconv_out.append(gtail.reshape(FFN_CONV - 1, bsz, -1).transpose(1, 0, 2))
        x = matmul(hcat, p["ffn_w_down"], i, res=x, tn=256)

    y = rmsnorm(x, p["norm_final"].reshape(1, d), 0, F32)
    ssm_new = jnp.stack(ssm_out) if prompt else ssm_all
    return y, ssm_new, jnp.stack(sconv_out), jnp.stack(pool_out), jnp.stack(fconv_out)


def kernel(x_prompt, x_sample, mem_prompt, cache_mem_k, cache_mem_v, state_ssm, state_ssm_conv, state_pool, state_ffn_conv, norm_mix, norm_xattn, norm_ffn, norm_mem, norm_final, ssd_w_in, ssd_conv_w, ssd_conv_b, ssd_dt_bias, ssd_a_log, ssd_d, ssd_norm, ssd_w_out, pool_w_in, pool_w_grp, pool_scale, pool_w_out, xa_w_q, xa_w_k, xa_w_v, xa_w_o, ffn_w_gate, ffn_conv_w, ffn_conv_b, ffn_w_up, ffn_w_down):
    p = dict(norm_mix=norm_mix, norm_xattn=norm_xattn, norm_ffn=norm_ffn, norm_final=norm_final,
             ssd_w_in=ssd_w_in, ssd_conv_w=ssd_conv_w, ssd_conv_b=ssd_conv_b, ssd_dt_bias=ssd_dt_bias,
             ssd_a_log=ssd_a_log, ssd_d=ssd_d, ssd_norm=ssd_norm, ssd_w_out=ssd_w_out,
             pool_w_in=pool_w_in, pool_w_grp=pool_w_grp, pool_scale=pool_scale, pool_w_out=pool_w_out,
             xa_w_q=xa_w_q, xa_w_o=xa_w_o, ffn_w_gate=ffn_w_gate, ffn_conv_w=ffn_conv_w,
             ffn_conv_b=ffn_conv_b, ffn_w_up=ffn_w_up, ffn_w_down=ffn_w_down)
    sw = _ssd_weights(p)
    depth = norm_mix.shape[0]
    bp, n_p, d = x_prompt.shape
    bs, n_s, _ = x_sample.shape
    n_mem = mem_prompt.shape[1]
    hd = d // XA_HEADS

    mem_k_p, mem_v_p = mem_kv(mem_prompt.reshape(bp * n_mem, d), norm_mem, xa_w_k, xa_w_v)
    mem_k_p = mem_k_p.reshape(depth, bp, n_mem, d)
    mem_v_p = mem_v_p.reshape(depth, bp, n_mem, d)

    y_p, ssm_p, sconv_p, pool_p, fconv_p = _trunk(
        x_prompt.reshape(bp * n_p, d), p, sw, prompt=True, bsz=bp, mem_k=mem_k_p, mem_v=mem_v_p)

    xs_tm = x_sample.transpose(1, 0, 2).reshape(n_s * bs, d)
    y_s, ssm_s, sconv_s, pool_s, fconv_s = _trunk(
        xs_tm, p, sw, prompt=False, bsz=bs,
        mem_k=cache_mem_k, mem_v=cache_mem_v,
        state_ssm=state_ssm, state_ssm_conv=state_ssm_conv, state_pool=state_pool,
        state_ffn_conv=state_ffn_conv)
    y_s = y_s.reshape(n_s, bs, d).transpose(1, 0, 2)

    kv_shape = (depth, bp, n_mem, XA_HEADS, hd)
    return (y_p.reshape(bp, n_p, d), y_s, mem_k_p.reshape(kv_shape), mem_v_p.reshape(kv_shape),
            ssm_p, sconv_p, pool_p, fconv_p, ssm_s, sconv_s, pool_s, fconv_s)
```

```python
import functools

import jax
import jax.numpy as jnp
from jax import lax
from jax.experimental import pallas as pl
from jax.experimental.pallas import tpu as pltpu

F32 = jnp.float32
BF16 = jnp.bfloat16

EPS = 1e-6
PAST_LEN = 16384
POOL_WINDOWS = (2, 4, 8, 16)

SSD_HEAD_DIM = 64
SSD_HEADS = 64
SSD_GROUPS = 8
SSD_HPG = SSD_HEADS // SSD_GROUPS
D_STATE = 128
D_INNER = SSD_HEADS * SSD_HEAD_DIM
SSD_GDIM = D_INNER // SSD_GROUPS
SSD_BC = SSD_GROUPS * D_STATE
SSD_CONV_DIM = D_INNER + 2 * SSD_BC
SSD_CONV = 4
SSD_CHUNK = 128
HEADS_PAD = 128

XA_HEADS = 4
FFN_CONV = 3

SUBLANES_V7X = 8
VMEM_LIMIT_V7X = 50 * 1024 * 1024


def _cparams(*sem):
    return pltpu.CompilerParams(dimension_semantics=sem, vmem_limit_bytes=VMEM_LIMIT_V7X)


def _silu(x):
    return x * (0.5 * jnp.tanh(0.5 * x) + 0.5)


def _softplus(x):
    return jnp.maximum(x, 0.0) + jnp.log(1.0 + jnp.exp(-jnp.abs(x)))


def _dot(a, b):
    return jnp.dot(a, b, preferred_element_type=F32)


def _dot_nt(a, b):
    return lax.dot_general(a, b, (((1,), (1,)), ((), ())), preferred_element_type=F32)


def _dot_tn(a, b):
    return lax.dot_general(a, b, (((0,), (0,)), ((), ())), preferred_element_type=F32)


def _split3(x):
    hi = x.astype(BF16)
    r1 = x - hi.astype(F32)
    mid = r1.astype(BF16)
    lo = (r1 - mid.astype(F32)).astype(BF16)
    return hi, mid, lo


def _dot_split_rhs01(parts, e01):
    return _dot(parts[0], e01) + _dot(parts[1], e01) + _dot(parts[2], e01)


def _dot_exact_rhs01(x, e01):
    return _dot_split_rhs01(_split3(x), e01)


def _dot_exact_lhs01(e01, x):
    hi, mid, lo = _split3(x)
    return _dot(e01, hi) + _dot(e01, mid) + _dot(e01, lo)


def _head_expand_matrix(first_head, width):
    row = lax.broadcasted_iota(jnp.int32, (HEADS_PAD, width), 0)
    lane = lax.broadcasted_iota(jnp.int32, (HEADS_PAD, width), 1)
    return jnp.where(row == first_head + lane // SSD_HEAD_DIM, 1.0, 0.0).astype(BF16)


def _rms_kernel(x_ref, g_ref, o_ref):
    x = x_ref[...]
    ms = jnp.mean(x * x, axis=-1, keepdims=True)
    o_ref[...] = (x * lax.rsqrt(ms + EPS) * g_ref[...]).astype(o_ref.dtype)


def rmsnorm(x, g_stack, layer, out_dtype):
    m, d = x.shape
    tm = min(m, 1024)
    g3 = g_stack.reshape(-1, 1, d)
    return pl.pallas_call(
        _rms_kernel,
        grid=(m // tm,),
        in_specs=[pl.BlockSpec((tm, d), lambda i: (i, 0)),
                  pl.BlockSpec((None, 1, d), lambda i: (layer, 0, 0))],
        out_specs=pl.BlockSpec((tm, d), lambda i: (i, 0)),
        out_shape=jax.ShapeDtypeStruct((m, d), out_dtype),
        compiler_params=_cparams("parallel"),
        name="rmsnorm",
    )(x, g3)


def _mm_kernel(x_ref, w_ref, *rest, has_res, w_is_nk):
    o_ref = rest[-1]
    w = w_ref[...].astype(BF16)
    acc = _dot_nt(x_ref[...], w) if w_is_nk else _dot(x_ref[...], w)
    if has_res:
        acc = acc + rest[0][...]
    o_ref[...] = acc.astype(o_ref.dtype)


def matmul(x, w_stack, layer, *, ncols=None, res=None, out_dtype=F32, tn=512, w_is_nk=False):
    m, k = x.shape
    n = ncols or w_stack.shape[1 if w_is_nk else 2]
    tm = 2048 if (k <= 2048 and m % 2048 == 0) else min(m, 1024)
    tn = min(tn, n)
    assert m % tm == 0 and n % tn == 0
    w_spec = (pl.BlockSpec((None, tn, k), lambda i, j: (layer, j, 0)) if w_is_nk
              else pl.BlockSpec((None, k, tn), lambda i, j: (layer, 0, j)))
    in_specs = [pl.BlockSpec((tm, k), lambda i, j: (i, 0)), w_spec]
    args = [x, w_stack]
    if res is not None:
        in_specs.append(pl.BlockSpec((tm, tn), lambda i, j: (i, j)))
        args.append(res)
    return pl.pallas_call(
        functools.partial(_mm_kernel, has_res=res is not None, w_is_nk=w_is_nk),
        grid=(m // tm, n // tn),
        in_specs=in_specs,
        out_specs=pl.BlockSpec((tm, tn), lambda i, j: (i, j)),
        out_shape=jax.ShapeDtypeStruct((m, n), out_dtype),
        compiler_params=_cparams("parallel", "arbitrary"),
        name="matmul",
    )(*args)


def _norm_mm_kernel(x_ref, g_ref, w_ref, o_ref, xn_ref):
    @pl.when(pl.program_id(1) == 0)
    def _():
        _rms_kernel(x_ref, g_ref, xn_ref)

    o_ref[...] = _dot(xn_ref[...], w_ref[...].astype(BF16)).astype(o_ref.dtype)


def norm_matmul(x, g_stack, w_stack, layer, *, out_dtype, tn=512):
    m, k = x.shape
    n = w_stack.shape[2]
    tm = min(m, 1024)
    return pl.pallas_call(
        _norm_mm_kernel,
        grid=(m // tm, n // tn),
        in_specs=[pl.BlockSpec((tm, k), lambda i, j: (i, 0)),
                  pl.BlockSpec((None, 1, k), lambda i, j: (layer, 0, 0)),
                  pl.BlockSpec((None, k, tn), lambda i, j: (layer, 0, j))],
        out_specs=pl.BlockSpec((tm, tn), lambda i, j: (i, j)),
        out_shape=jax.ShapeDtypeStruct((m, n), out_dtype),
        scratch_shapes=[pltpu.VMEM((tm, k), BF16)],
        compiler_params=_cparams("parallel", "arbitrary"),
        name="norm_matmul",
    )(x, g_stack.reshape(-1, 1, k), w_stack)


def _mem_kv_kernel(x_ref, g_ref, wk_ref, wv_ref, k_ref, v_ref, xn_ref):
    @pl.when(pl.program_id(1) == 0)
    def _():
        x = x_ref[...]
        ms = jnp.mean(x * x, axis=-1, keepdims=True)
        xn_ref[...] = (x * lax.rsqrt(ms + EPS) * g_ref[...]).astype(BF16)

    xn = xn_ref[...]
    k_ref[...] = _dot(xn, wk_ref[...].astype(BF16))
    v_ref[...] = _dot(xn, wv_ref[...].astype(BF16))


def mem_kv(mem, g_stack, w_k, w_v, tn=256):
    m, d = mem.shape
    nl = w_k.shape[0]
    w_spec = pl.BlockSpec((None, d, tn), lambda l, j: (l, 0, j))
    o_spec = pl.BlockSpec((None, m, tn), lambda l, j: (l, 0, j))
    out = jax.ShapeDtypeStruct((nl, m, d), F32)
    return pl.pallas_call(
        _mem_kv_kernel,
        grid=(nl, d // tn),
        in_specs=[pl.BlockSpec((m, d), lambda l, j: (0, 0)),
                  pl.BlockSpec((None, 1, d), lambda l, j: (l, 0, 0)), w_spec, w_spec],
        out_specs=[o_spec, o_spec],
        out_shape=[out, out],
        scratch_shapes=[pltpu.VMEM((m, d), BF16)],
        compiler_params=_cparams("arbitrary", "arbitrary"),
        name="mem_kv",
    )(mem, g_stack.reshape(nl, 1, d), w_k, w_v)


def _ssd_chunk_kernel(z0_ref, z1_ref, x0_ref, x1_ref, bc_ref, dtr_ref, cw_ref, cb_ref,
                      dtb_ref, alog_ref, dskip_ref, gn_ref,
                      y_ref, h_ref, tail_ref, ext_ref, xbc_ref, yg_ref):
    q = x0_ref.shape[0]
    hist = SUBLANES_V7X
    c = pl.program_id(1)

    @pl.when(c == 0)
    def _():
        ext_ref[0:hist, :] = jnp.zeros((hist, SSD_CONV_DIM), F32)
        h_ref[...] = jnp.zeros(h_ref.shape, F32)

    half = D_INNER // 2
    ext_ref[hist:hist + q, 0:half] = x0_ref[...]
    ext_ref[hist:hist + q, half:D_INNER] = x1_ref[...]
    ext_ref[hist:hist + q, D_INNER:SSD_CONV_DIM] = bc_ref[...]

    for s in range(SSD_CONV_DIM // SSD_GDIM):
        cols = slice(s * SSD_GDIM, (s + 1) * SSD_GDIM)
        full = ext_ref[:, cols]
        acc = cb_ref[:, cols] + full[hist:hist + q] * cw_ref[SSD_CONV - 1:SSD_CONV, cols]
        for k in range(SSD_CONV - 1):
            shifted = pltpu.roll(full, SSD_CONV - 1 - k, axis=0)
            acc = acc + shifted[hist:hist + q] * cw_ref[k:k + 1, cols]
        xbc_ref[:, cols] = _silu(acc)
    ext_ref[0:hist, :] = ext_ref[q:q + hist, :]
    tail_ref[...] = ext_ref[0:hist, :]

    dt = _softplus(dtr_ref[...] + dtb_ref[...])
    da = dt * (-jnp.exp(alog_ref[...]))
    ii = lax.broadcasted_iota(jnp.int32, (q, q), 0)
    jj = lax.broadcasted_iota(jnp.int32, (q, q), 1)
    causal = ii >= jj
    tri = jnp.where(causal, 1.0, 0.0).astype(BF16)
    acum = _dot_exact_lhs01(tri, da)
    acum_t = acum.T
    a_last = acum[q - 1:q, :]
    stack = _split3(jnp.concatenate([dt, jnp.exp(acum), jnp.exp(a_last - acum) * dt], axis=0))
    lane = lax.broadcasted_iota(jnp.int32, (q, 2 * SSD_HEAD_DIM), 1)
    first_of_pair = lane < SSD_HEAD_DIM
    expand = _head_expand_matrix(0, D_INNER)

    for g in range(SSD_GROUPS):
        gcols = slice(g * SSD_GDIM, (g + 1) * SSD_GDIM)
        ex = _dot_split_rhs01(stack, expand[:, gcols])
        dt_e, a_e, wl_e = ex[0:q], ex[q:2 * q], ex[2 * q:3 * q]
        xs = xbc_ref[:, gcols]
        bm = xbc_ref[:, D_INNER + g * D_STATE:D_INNER + (g + 1) * D_STATE].astype(BF16)
        cm = xbc_ref[:, D_INNER + SSD_BC + g * D_STATE:D_INNER + SSD_BC + (g + 1) * D_STATE].astype(BF16)
        xdt = (xs * dt_e).astype(BF16)
        xw = (xs * wl_e).astype(BF16)
        cbm = _dot_nt(cm, bm)
        hg = h_ref[g * SSD_HPG:(g + 1) * SSD_HPG].reshape(SSD_HPG * SSD_HEAD_DIM, D_STATE)
        y_off = _dot_nt(cm, hg.astype(BF16))
        s_new = _dot_tn(xw, bm)
        for pr in range(SSD_HPG // 2):
            pcols = slice(pr * 2 * SSD_HEAD_DIM, (pr + 1) * 2 * SSD_HEAD_DIM)
            ys = []
            for hh in range(2):
                h = g * SSD_HPG + pr * 2 + hh
                diff = acum[:, h:h + 1] - acum_t[h:h + 1, :]
                decay = jnp.exp(jnp.where(causal, diff, -jnp.inf))
                ys.append(_dot((cbm * decay).astype(BF16), xdt[:, pcols]))
            yg_ref[:, pcols] = jnp.where(first_of_pair, ys[0], ys[1]) + a_e[:, pcols] * y_off[:, pcols]
        for r in range(SSD_HPG):
            h = g * SSD_HPG + r
            rows = slice(r * SSD_HEAD_DIM, (r + 1) * SSD_HEAD_DIM)
            h_ref[h] = hg[rows] * jnp.exp(acum_t[h:h + 1, q - 1:q]) + s_new[rows]
        z_ref = z0_ref if g < SSD_GROUPS // 2 else z1_ref
        zc = slice((g % (SSD_GROUPS // 2)) * SSD_GDIM, (g % (SSD_GROUPS // 2) + 1) * SSD_GDIM)
        yv = (yg_ref[...] + xs * dskip_ref[:, gcols]) * _silu(z_ref[:, zc])
        ms = jnp.mean(yv * yv, axis=-1, keepdims=True)
        y_ref[:, gcols] = (yv * lax.rsqrt(ms + EPS) * gn_ref[:, gcols]).astype(y_ref.dtype)


def ssd_prompt(proj, dt_raw, bsz, layer, w):
    m = proj.shape[0]
    n = m // bsz
    q = SSD_CHUNK
    nc = n // q
    half = D_INNER // 2
    row = lambda b, c: b * nc + c
    pspec = lambda cb: pl.BlockSpec((q, half), lambda b, c: (row(b, c), cb))
    vec = lambda width: pl.BlockSpec((None, 1, width), lambda b, c: (layer, 0, 0))
    y, h, tail = pl.pallas_call(
        _ssd_chunk_kernel,
        grid=(bsz, nc),
        in_specs=[pspec(0), pspec(1), pspec(2), pspec(3), pspec(4),
                  pl.BlockSpec((q, HEADS_PAD), lambda b, c: (row(b, c), 0)),
                  pl.BlockSpec((None, SSD_CONV, SSD_CONV_DIM), lambda b, c: (layer, 0, 0)),
                  vec(SSD_CONV_DIM), vec(HEADS_PAD), vec(HEADS_PAD),
                  vec(D_INNER), vec(D_INNER)],
        out_specs=[pl.BlockSpec((q, D_INNER), lambda b, c: (row(b, c), 0)),
                   pl.BlockSpec((None, SSD_HEADS, SSD_HEAD_DIM, D_STATE), lambda b, c: (b, 0, 0, 0)),
                   pl.BlockSpec((None, SUBLANES_V7X, SSD_CONV_DIM), lambda b, c: (b, 0, 0))],
        out_shape=[jax.ShapeDtypeStruct((m, D_INNER), BF16),
                   jax.ShapeDtypeStruct((bsz, SSD_HEADS, SSD_HEAD_DIM, D_STATE), F32),
                   jax.ShapeDtypeStruct((bsz, SUBLANES_V7X, SSD_CONV_DIM), F32)],
        scratch_shapes=[pltpu.VMEM((q + 2 * SUBLANES_V7X, SSD_CONV_DIM), F32),
                        pltpu.VMEM((q, SSD_CONV_DIM), F32),
                        pltpu.VMEM((q, SSD_GDIM), F32)],
        compiler_params=_cparams("parallel", "arbitrary"),
        name="ssd_chunk",
    )(proj, proj, proj, proj, proj, dt_raw, w["conv_w"], w["conv_b"],
      w["dt_bias"], w["a_log"], w["d_skip"], w["g_norm"])
    return y, h, tail[:, SUBLANES_V7X - (SSD_CONV - 1):]


def _conv_time_major(st_ref, x_ref, w_ref, b_ref, nb):
    width = w_ref.shape[0]
    hist = (width - 1) * nb
    rows = x_ref.shape[0]
    acc = b_ref[...]
    for k in range(width):
        parts = []
        if k * nb < hist:
            parts.append(st_ref[k * nb:hist, :])
        take = rows - (hist - k * nb)
        if take > 0:
            parts.append(x_ref[0:take, :])
        piece = parts[0] if len(parts) == 1 else jnp.concatenate(parts, axis=0)
        acc = acc + piece[0:rows] * w_ref[k:k + 1, :]
    return _silu(acc)


def _ssd_sample_prep_kernel(xr_ref, br_ref, cr_ref, z_ref, sx_ref, sb_ref, sc_ref, wx_ref, wb_ref, wc_ref,
                            bx_ref, bb_ref, bcb_ref, dtr_ref, dtb_ref, alog_ref, dskip_ref,
                            y2_ref, ae_ref, xw_ref, bm_ref, cm_ref, zo_ref, dl_ref, *, nb, nt):
    g = pl.program_id(0)
    xs = _conv_time_major(sx_ref, xr_ref, wx_ref, bx_ref, nb)
    bm = _conv_time_major(sb_ref, br_ref, wb_ref, bb_ref, nb)
    cm = _conv_time_major(sc_ref, cr_ref, wc_ref, bcb_ref, nb)
    bmr = bm.astype(BF16).astype(F32)
    cmr = cm.astype(BF16).astype(F32)

    dt = _softplus(dtr_ref[...] + dtb_ref[...])
    da = dt * (-jnp.exp(alog_ref[...]))
    sl = lambda a, t: a[t * nb:(t + 1) * nb]
    acums = [sl(da, 0)]
    for t in range(1, nt):
        acums.append(acums[-1] + sl(da, t))
    dl_ref[...] = jnp.exp(acums[-1])

    row = lax.broadcasted_iota(jnp.int32, (HEADS_PAD, SSD_GDIM), 0)
    lane = lax.broadcasted_iota(jnp.int32, (HEADS_PAD, SSD_GDIM), 1)
    e01 = jnp.where(row == g * SSD_HPG + lane // SSD_HEAD_DIM, 1.0, 0.0).astype(BF16)
    ex = _dot_exact_rhs01(jnp.concatenate(acums + [dt], axis=0), e01)
    a_e = [sl(ex, t) for t in range(nt)]
    d_e = [sl(ex, nt + t) for t in range(nt)]
    for t in range(nt):
        y2 = sl(xs, t) * dskip_ref[...]
        for j in range(t + 1):
            cb = jnp.sum(sl(cmr, t) * sl(bmr, j), axis=-1, keepdims=True)
            y2 = y2 + jnp.exp(a_e[t] - a_e[j]) * d_e[j] * cb * sl(xs, j)
        y2_ref[:, t, :] = y2
        ae_ref[:, t, :] = jnp.exp(a_e[t])
        xw_ref[:, t, :] = jnp.exp(a_e[nt - 1] - a_e[t]) * d_e[t] * sl(xs, t)
        bm_ref[:, t, :] = sl(bm, t)
        cm_ref[:, t, :] = sl(cm, t)
        zo_ref[:, t, :] = z_ref[t * nb:(t + 1) * nb, :]


def _ssd_sample_state_kernel(dl_ref, h0_ref, cm_ref, bm_ref, xw_ref, ae_ref, y2_ref, z_ref, gn_ref, *rest,
                             n_prev):
    if n_prev:
        prev_ref, y_ref, h_ref = rest
        for l in range(n_prev):
            h_ref[l] = prev_ref[l]
    else:
        y_ref, h_ref = rest
    nbb = h0_ref.shape[0]
    for bb in range(nbb):
        b = pl.program_id(0) * nbb + bb
        for g in range(SSD_GROUPS):
            gcols = slice(g * SSD_GDIM, (g + 1) * SSD_GDIM)
            scols = slice(g * D_STATE, (g + 1) * D_STATE)
            hg = h0_ref[bb, g * SSD_HPG:(g + 1) * SSD_HPG].reshape(SSD_HPG * SSD_HEAD_DIM, D_STATE)
            y_off = _dot_nt(cm_ref[bb, :, scols].astype(BF16), hg.astype(BF16))
            s_new = _dot_tn(xw_ref[bb, :, gcols].astype(BF16), bm_ref[bb, :, scols].astype(BF16))
            for r in range(SSD_HPG):
                h = g * SSD_HPG + r
                rows = slice(r * SSD_HEAD_DIM, (r + 1) * SSD_HEAD_DIM)
                h_ref[n_prev, bb, h] = hg[rows] * dl_ref[b, h] + s_new[rows]
            yv = (ae_ref[bb, :, gcols] * y_off + y2_ref[bb, :, gcols]) * _silu(z_ref[bb, :, gcols])
            ms = jnp.mean(yv * yv, axis=-1, keepdims=True)
            y_ref[bb, :, gcols] = yv * lax.rsqrt(ms + EPS) * gn_ref[:, gcols]


def ssd_sample(proj, dt_raw, conv_state, h0_stack, h_prev, nb, layer, w):
    m = proj.shape[0]
    nt = m // nb
    hist = (SSD_CONV - 1) * nb
    xblk = D_INNER // SSD_GDIM
    bblk = 2 * D_INNER // D_STATE
    cblk = bblk + SSD_GROUPS
    sbb = D_INNER // D_STATE
    scb = sbb + SSD_GROUPS
    vec = lambda width: pl.BlockSpec((None, 1, width), lambda g: (layer, 0, 0))
    bm_out = lambda width: jax.ShapeDtypeStruct((nb, nt, width), F32)
    bm_spec = lambda width: pl.BlockSpec((nb, nt, width), lambda g: (0, 0, g))
    y2, a_e, xw, bm, cm, z_bm, dlast = pl.pallas_call(
        functools.partial(_ssd_sample_prep_kernel, nb=nb, nt=nt),
        grid=(SSD_GROUPS,),
        in_specs=[pl.BlockSpec((m, SSD_GDIM), lambda g: (0, xblk + g)),
                  pl.BlockSpec((m, D_STATE), lambda g: (0, bblk + g)),
                  pl.BlockSpec((m, D_STATE), lambda g: (0, cblk + g)),
                  pl.BlockSpec((m, SSD_GDIM), lambda g: (0, g)),
                  pl.BlockSpec((hist, SSD_GDIM), lambda g: (0, g)),
                  pl.BlockSpec((hist, D_STATE), lambda g: (0, sbb + g)),
                  pl.BlockSpec((hist, D_STATE), lambda g: (0, scb + g)),
                  pl.BlockSpec((None, SSD_CONV, SSD_GDIM), lambda g: (layer, 0, g)),
                  pl.BlockSpec((None, SSD_CONV, D_STATE), lambda g: (layer, 0, sbb + g)),
                  pl.BlockSpec((None, SSD_CONV, D_STATE), lambda g: (layer, 0, scb + g)),
                  pl.BlockSpec((None, 1, SSD_GDIM), lambda g: (layer, 0, g)),
                  pl.BlockSpec((None, 1, D_STATE), lambda g: (layer, 0, sbb + g)),
                  pl.BlockSpec((None, 1, D_STATE), lambda g: (layer, 0, scb + g)),
                  pl.BlockSpec((m, HEADS_PAD), lambda g: (0, 0)),
                  vec(HEADS_PAD), vec(HEADS_PAD),
                  pl.BlockSpec((None, 1, SSD_GDIM), lambda g: (layer, 0, g))],
        out_specs=[bm_spec(SSD_GDIM), bm_spec(SSD_GDIM), bm_spec(SSD_GDIM), bm_spec(D_STATE), bm_spec(D_STATE),
                   bm_spec(SSD_GDIM), pl.BlockSpec((nb, HEADS_PAD), lambda g: (0, 0))],
        out_shape=[bm_out(D_INNER), bm_out(D_INNER), bm_out(D_INNER), bm_out(SSD_BC), bm_out(SSD_BC),
                   bm_out(D_INNER), jax.ShapeDtypeStruct((nb, HEADS_PAD), F32)],
        compiler_params=_cparams("arbitrary"),
        name="ssd_sample_prep",
    )(proj, proj, proj, proj, conv_state, conv_state, conv_state, w["conv_w"], w["conv_w"], w["conv_w"],
      w["conv_b"], w["conv_b"], w["conv_b"], dt_raw, w["dt_bias"], w["a_log"], w["d_skip"])

    n_prev = 0 if h_prev is None else h_prev.shape[0]
    state_dims = (SSD_HEADS, SSD_HEAD_DIM, D_STATE)
    nbb = 2 if nb % 2 == 0 else 1
    per_b = lambda width: pl.BlockSpec((nbb, nt, width), lambda b: (b, 0, 0))
    in_specs = [pl.BlockSpec(memory_space=pltpu.SMEM),
                pl.BlockSpec((None, nbb) + state_dims, lambda b: (layer, b, 0, 0, 0)),
                per_b(SSD_BC), per_b(SSD_BC), per_b(D_INNER), per_b(D_INNER), per_b(D_INNER), per_b(D_INNER),
                pl.BlockSpec((None, 1, D_INNER), lambda b: (layer, 0, 0))]
    args = [dlast[:, :SSD_HEADS], h0_stack, cm, bm, xw, a_e, y2, z_bm, w["g_norm"]]
    if n_prev:
        in_specs.append(pl.BlockSpec((n_prev, nbb) + state_dims, lambda b: (0, b, 0, 0, 0)))
        args.append(h_prev)
    y, h = pl.pallas_call(
        functools.partial(_ssd_sample_state_kernel, n_prev=n_prev),
        grid=(nb // nbb,),
        in_specs=in_specs,
        out_specs=[per_b(D_INNER),
                   pl.BlockSpec((n_prev + 1, nbb) + state_dims, lambda b: (0, b, 0, 0, 0))],
        out_shape=[jax.ShapeDtypeStruct((nb, nt, D_INNER), F32),
                   jax.ShapeDtypeStruct((n_prev + 1, nb) + state_dims, F32)],
        compiler_params=_cparams("parallel"),
        name="ssd_sample_state",
    )(*args)
    y_tm = y.transpose(1, 0, 2).reshape(m, D_INNER).astype(BF16)
    return y_tm, h


def _pool_kernel(*refs, tm, stride, halo, tail, tiles_per_seq, pos0, has_state):
    if has_state:
        x_ref, gn_ref, win_ref, wg_ref, sc_ref, st_ref, mixed_ref, vtail_ref, u_ref, ext_ref = refs
    else:
        x_ref, gn_ref, win_ref, wg_ref, sc_ref, mixed_ref, vtail_ref, u_ref, ext_ref, carry_ref = refs
    i = pl.program_id(0)
    g = pl.program_id(1)
    width = ext_ref.shape[1]

    @pl.when(g == 0)
    def _():
        _rms_kernel(x_ref, gn_ref, u_ref)

    if has_state:
        ext_ref[0:halo, :] = st_ref[...]
    else:
        first = i % tiles_per_seq == 0

        @pl.when(first)
        def _():
            ext_ref[0:halo, :] = jnp.zeros((halo, width), F32)

        @pl.when(jnp.logical_not(first))
        def _():
            ext_ref[0:halo, :] = carry_ref[g]

    v = _dot(u_ref[...], win_ref[...].astype(BF16))
    ext_ref[halo:halo + tm, :] = v
    vtail_ref[...] = v[tm - tail:tm]
    if not has_state:
        carry_ref[g] = v[tm - halo:tm]
    t0 = (i % tiles_per_seq) * (tm // stride)
    t = t0 + lax.broadcasted_iota(jnp.int32, (tm, width), 0) // stride

    for gi, win in enumerate(POOL_WINDOWS):
        @pl.when(g == gi)
        def _(win=win):
            s = v
            for d in range(1, win):
                s = s + ext_ref[pl.ds(halo - d * stride, tm), :]
            cnt = jnp.minimum(pos0 + t + 1, win).astype(F32)
            p = (s / cnt - v).astype(BF16)
            mixed_ref[...] = (_dot(p, wg_ref[...].astype(BF16)) * sc_ref[...]).astype(mixed_ref.dtype)


def pool_mix(x, g_stack, g_layer, w_in, w_grp, scale3, layer, *, n_seq, stride, pos0, state=None):
    m, d = x.shape
    ng = len(POOL_WINDOWS)
    gdim = w_in.shape[-1] // ng
    has_state = state is not None
    if has_state:
        tm, halo, tail, tiles_per_seq = m, state.shape[0], m, 1
    else:
        tm, halo, tail = min(m // n_seq, 1024), 2 * SUBLANES_V7X, 2 * SUBLANES_V7X
        tiles_per_seq = (m // n_seq) // tm
    in_specs = [pl.BlockSpec((tm, d), lambda i, g: (i, 0)),
                pl.BlockSpec((None, 1, d), lambda i, g: (g_layer, 0, 0)),
                pl.BlockSpec((None, d, gdim), lambda i, g: (layer, 0, g)),
                pl.BlockSpec((None, None, gdim, gdim), lambda i, g: (layer, g, 0, 0)),
                pl.BlockSpec((None, 1, gdim), lambda i, g: (layer, 0, g))]
    args = [x, g_stack.reshape(-1, 1, d), w_in, w_grp, scale3]
    scratch = [pltpu.VMEM((tm, d), BF16), pltpu.VMEM((halo + tm, gdim), F32)]
    if has_state:
        in_specs.append(pl.BlockSpec((halo, gdim), lambda i, g: (0, g)))
        args.append(state)
    else:
        scratch.append(pltpu.VMEM((ng, halo, gdim), F32))
    return pl.pallas_call(
        functools.partial(_pool_kernel, tm=tm, stride=stride, halo=halo, tail=tail,
                          tiles_per_seq=tiles_per_seq, pos0=pos0, has_state=has_state),
        grid=(m // tm, ng),
        in_specs=in_specs,
        out_specs=[pl.BlockSpec((tm, gdim), lambda i, g: (i, g)),
                   pl.BlockSpec((tail, gdim), lambda i, g: (i, g))],
        out_shape=[jax.ShapeDtypeStruct((m, ng * gdim), BF16),
                   jax.ShapeDtypeStruct((m // tm * tail, ng * gdim), F32)],
        scratch_shapes=scratch,
        compiler_params=_cparams("arbitrary", "arbitrary"),
        name="pool_mix",
    )(*args)


def _attn_kernel(q_ref, k_ref, v_ref, o_ref):
    hd = q_ref.shape[-1] // XA_HEADS
    scale = hd ** -0.5
    for h in range(XA_HEADS):
        cols = slice(h * hd, (h + 1) * hd)
        s = _dot_nt(q_ref[:, cols], k_ref[:, cols].astype(BF16)) * scale
        e = jnp.exp(s - jnp.max(s, axis=-1, keepdims=True))
        p = e * (1.0 / jnp.sum(e, axis=-1, keepdims=True))
        o_ref[:, cols] = _dot(p.astype(BF16), v_ref[:, cols].astype(BF16)).astype(o_ref.dtype)


def attention(q, k, v, kv_layer):
    bsz, n, d = q.shape
    n_mem = k.shape[2]
    tq = min(n, 1024)
    kv_spec = pl.BlockSpec((None, None, n_mem, d), lambda b, i: (kv_layer, b, 0, 0))
    return pl.pallas_call(
        _attn_kernel,
        grid=(bsz, n // tq),
        in_specs=[pl.BlockSpec((None, tq, d), lambda b, i: (b, i, 0)), kv_spec, kv_spec],
        out_specs=pl.BlockSpec((None, tq, d), lambda b, i: (b, i, 0)),
        out_shape=jax.ShapeDtypeStruct((bsz, n, d), BF16),
        compiler_params=_cparams("parallel", "arbitrary"),
        name="mem_attention",
    )(q, k, v)


def _attn_cache_kernel(q_ref, k_ref, v_ref, o_ref, *, tp):
    nbb, n_mem, heads, hd = k_ref.shape
    rows = n_mem * heads
    for bb in range(nbb):
        kf = k_ref[bb].reshape(rows, hd).astype(BF16)
        vf = v_ref[bb].reshape(rows, hd).astype(BF16)
        s_t = _dot_nt(kf, q_ref[bb]) * (hd ** -0.5)
        r = lax.broadcasted_iota(jnp.int32, s_t.shape, 0)
        c = lax.broadcasted_iota(jnp.int32, s_t.shape, 1)
        s_t = jnp.where(r % heads == c // tp, s_t, -jnp.inf)
        e = jnp.exp(s_t - jnp.max(s_t, axis=0, keepdims=True))
        p_t = e * (1.0 / jnp.sum(e, axis=0, keepdims=True))
        o_ref[bb] = _dot_tn(p_t.astype(BF16), vf).astype(o_ref.dtype)


def attention_cache(q, k, v, kv_layer, tp):
    bsz = q.shape[0]
    nbb = 4 if bsz % 4 == 0 else 1
    kv_spec = pl.BlockSpec((None, nbb) + k.shape[2:], lambda b: (kv_layer, b, 0, 0, 0))
    q_spec = pl.BlockSpec((nbb,) + q.shape[1:], lambda b: (b, 0, 0))
    return pl.pallas_call(
        functools.partial(_attn_cache_kernel, tp=tp),
        grid=(bsz // nbb,),
        in_specs=[q_spec, kv_spec, kv_spec],
        out_specs=q_spec,
        out_shape=jax.ShapeDtypeStruct(q.shape, BF16),
        compiler_params=_cparams("parallel"),
        name="cache_attention",
    )(q, k, v)


def _ffn_gate_kernel(*refs, tm, stride, halo, tail, tiles_per_seq, has_state):
    if has_state:
        u_ref, wg_ref, wu_ref, cw_ref, cb_ref, st_ref, h_ref, gtail_ref, ext_ref = refs
    else:
        u_ref, wg_ref, wu_ref, cw_ref, cb_ref, h_ref, gtail_ref, ext_ref, carry_ref = refs
    i = pl.program_id(0)
    j = pl.program_id(1)
    u = u_ref[...]
    if has_state:
        ext_ref[0:halo, :] = st_ref[...]
    else:
        first = i % tiles_per_seq == 0

        @pl.when(first)
        def _():
            ext_ref[0:halo, :] = jnp.zeros((halo, ext_ref.shape[1]), F32)

        @pl.when(jnp.logical_not(first))
        def _():
            ext_ref[0:halo, :] = carry_ref[j]

    gate = _dot(u, wg_ref[...].astype(BF16))
    up = _dot(u, wu_ref[...].astype(BF16))
    ext_ref[halo:halo + tm, :] = gate
    acc = cb_ref[...] + gate * cw_ref[FFN_CONV - 1:FFN_CONV, :]
    for k in range(FFN_CONV - 1):
        acc = acc + ext_ref[pl.ds(halo - (FFN_CONV - 1 - k) * stride, tm), :] * cw_ref[k:k + 1, :]
    h_ref[...] = (_silu(acc) * up).astype(h_ref.dtype)
    gtail_ref[...] = ext_ref[halo + tm - tail:halo + tm, :]
    if not has_state:
        carry_ref[j] = ext_ref[tm:tm + halo, :]


def ffn_gate(u, w_gate, w_up, conv_w, conv_b3, layer, *, n_seq, stride, state=None, tn=512):
    m, d = u.shape
    nf = w_gate.shape[-1]
    has_state = state is not None
    if has_state:
        tm, halo, tail, tiles_per_seq = m, state.shape[0], state.shape[0], 1
    else:
        tm, halo, tail = min(m // n_seq, 1024), SUBLANES_V7X, SUBLANES_V7X
        tiles_per_seq = (m // n_seq) // tm
    wspec = pl.BlockSpec((None, d, tn), lambda i, j: (layer, 0, j))
    in_specs = [pl.BlockSpec((tm, d), lambda i, j: (i, 0)), wspec, wspec,
                pl.BlockSpec((None, FFN_CONV, tn), lambda i, j: (layer, 0, j)),
                pl.BlockSpec((None, 1, tn), lambda i, j: (layer, 0, j))]
    args = [u, w_gate, w_up, conv_w, conv_b3]
    scratch = [pltpu.VMEM((halo + tm, tn), F32)]
    if has_state:
        in_specs.append(pl.BlockSpec((halo, tn), lambda i, j: (0, j)))
        args.append(state)
    else:
        scratch.append(pltpu.VMEM((nf // tn, halo, tn), F32))
    return pl.pallas_call(
        functools.partial(_ffn_gate_kernel, tm=tm, stride=stride, halo=halo, tail=tail,
                          tiles_per_seq=tiles_per_seq, has_state=has_state),
        grid=(m // tm, nf // tn),
        in_specs=in_specs,
        out_specs=[pl.BlockSpec((tm, tn), lambda i, j: (i, j)),
                   pl.BlockSpec((tail, tn), lambda i, j: (i, j))],
        out_shape=[jax.ShapeDtypeStruct((m, nf), BF16),
                   jax.ShapeDtypeStruct((m // tm * tail, nf), F32)],
        scratch_shapes=scratch,
        compiler_params=_cparams("arbitrary", "arbitrary"),
        name="ffn_gate",
    )(*args)


def _ssd_weights(p):
    pad = HEADS_PAD - SSD_HEADS
    nl = p["ssd_dt_bias"].shape[0]
    padv = lambda a: jnp.pad(a, ((0, 0), (0, pad)))
    dt_bias = padv(p["ssd_dt_bias"])
    a_log = padv(p["ssd_a_log"])
    return dict(
        w_in_nk=jnp.swapaxes(p["ssd_w_in"], 1, 2),
        w_dt=jnp.pad(p["ssd_w_in"][:, :, D_INNER + SSD_CONV_DIM:], ((0, 0), (0, 0), (0, pad))),
        conv_w=p["ssd_conv_w"],
        conv_b=p["ssd_conv_b"].reshape(nl, 1, SSD_CONV_DIM),
        dt_bias=dt_bias.reshape(nl, 1, HEADS_PAD),
        a_log=a_log.reshape(nl, 1, HEADS_PAD),
        d_skip=jnp.repeat(p["ssd_d"], SSD_HEAD_DIM, axis=1).reshape(nl, 1, D_INNER),
        g_norm=p["ssd_norm"].reshape(nl, 1, D_INNER))


def _trunk(x, p, sw, *, prompt, bsz, mem_k, mem_v, state_ssm=None, state_ssm_conv=None, state_pool=None,
           state_ffn_conv=None):
    m, d = x.shape
    n = m // bsz
    depth = p["norm_mix"].shape[0]
    stride = 1 if prompt else bsz
    n_seq = bsz if prompt else 1
    proj_cols = D_INNER + SSD_CONV_DIM
    ssm_out, sconv_out, pool_out, fconv_out = [], [], [], []
    ssm_all = None

    def time_major_state(s):
        return s.transpose(1, 0, 2).reshape(s.shape[1] * s.shape[0], s.shape[2])

    for i in range(depth):
        j = i // 2
        if i % 2 == 0:
            u = rmsnorm(x, p["norm_mix"], i, BF16)
            proj = matmul(u, sw["w_in_nk"], j, ncols=proj_cols, w_is_nk=True)
            dt_raw = matmul(u, sw["w_dt"], j)
            if prompt:
                y, h, sconv_tail = ssd_prompt(proj, dt_raw, bsz, j, sw)
                ssm_out.append(h)
                sconv_out.append(sconv_tail)
            else:
                y, ssm_all = ssd_sample(proj, dt_raw, time_major_state(state_ssm_conv[j]), state_ssm, ssm_all,
                                        bsz, j, sw)
                xbc = proj.reshape(n, bsz, proj_cols)[:, :, D_INNER:]
                ext = jnp.concatenate([state_ssm_conv[j], xbc.transpose(1, 0, 2)], axis=1)
                sconv_out.append(ext[:, -(SSD_CONV - 1):])
            x = matmul(y, p["ssd_w_out"], j, res=x)
        else:
            scale3 = p["pool_scale"].reshape(-1, 1, p["pool_scale"].shape[-1])
            if prompt:
                mixed, vtail = pool_mix(x, p["norm_mix"], i, p["pool_w_in"], p["pool_w_grp"], scale3, j,
                                        n_seq=n_seq, stride=1, pos0=0)
                nbuf = max(POOL_WINDOWS) - 1
                pool_out.append(vtail.reshape(bsz, -1, vtail.shape[-1])[:, -nbuf:])
            else:
                mixed, vtail = pool_mix(x, p["norm_mix"], i, p["pool_w_in"], p["pool_w_grp"], scale3, j,
                                        n_seq=n_seq, stride=bsz, pos0=PAST_LEN,
                                        state=time_major_state(state_pool[j]))
                v_bm = vtail.reshape(n, bsz, -1).transpose(1, 0, 2)
                nbuf = state_pool.shape[2]
                pool_out.append(jnp.concatenate([state_pool[j], v_bm], axis=1)[:, -nbuf:])
            x = matmul(mixed, p["pool_w_out"], j, res=x)

        qm = norm_matmul(x, p["norm_xattn"], p["xa_w_q"], i, out_dtype=BF16)
        if prompt:
            o = attention(qm.reshape(bsz, n, d), mem_k, mem_v, i).reshape(m, d)
        else:
            hd = d // XA_HEADS
            q4 = qm.reshape(n, bsz, XA_HEADS, hd).transpose(1, 2, 0, 3).reshape(bsz, XA_HEADS * n, hd)
            o4 = attention_cache(q4, mem_k, mem_v, i, n).reshape(bsz, XA_HEADS, n, hd)
            o = o4.transpose(2, 0, 1, 3).reshape(m, d)
        x = matmul(o, p["xa_w_o"], i, res=x)

        u = rmsnorm(x, p["norm_ffn"], i, BF16)
        cb3 = p["ffn_conv_b"].reshape(depth, 1, -1)
        if prompt:
            hcat, gtail = ffn_gate(u, p["ffn_w_gate"], p["ffn_w_up"], p["ffn_conv_w"], cb3, i, n_seq=n_seq,
                                   stride=1)
            fconv_out.append(gtail.reshape(bsz, -1, gtail.shape[-1])[:, -(FFN_CONV - 1):])
        else:
            hcat, gtail = ffn_gate(u, p["ffn_w_gate"], p["ffn_w_up"], p["ffn_conv_w"], cb3, i, n_seq=n_seq,
                                   stride=bsz, state=time_major_state(state_ffn_conv[i]))
            fconv_out.append(gtail.reshape(FFN_CONV - 1, bsz, -1).transpose(1, 0, 2))
        x = matmul(hcat, p["ffn_w_down"], i, res=x, tn=256)

    y = rmsnorm(x, p["norm_final"].reshape(1, d), 0, F32)
    ssm_new = jnp.stack(ssm_out) if prompt else ssm_all
    return y, ssm_new, jnp.stack(sconv_out), jnp.stack(pool_out), jnp.stack(fconv_out)


def kernel(x_prompt, x_sample, mem_prompt, cache_mem_k, cache_mem_v, state_ssm, state_ssm_conv, state_pool, state_ffn_conv, norm_mix, norm_xattn, norm_ffn, norm_mem, norm_final, ssd_w_in, ssd_conv_w, ssd_conv_b, ssd_dt_bias, ssd_a_log, ssd_d, ssd_norm, ssd_w_out, pool_w_in, pool_w_grp, pool_scale, pool_w_out, xa_w_q, xa_w_k, xa_w_v, xa_w_o, ffn_w_gate, ffn_conv_w, ffn_conv_b, ffn_w_up, ffn_w_down):
    p = dict(norm_mix=norm_mix, norm_xattn=norm_xattn, norm_ffn=norm_ffn, norm_final=norm_final,
             ssd_w_in=ssd_w_in, ssd_conv_w=ssd_conv_w, ssd_conv_b=ssd_conv_b, ssd_dt_bias=ssd_dt_bias,
             ssd_a_log=ssd_a_log, ssd_d=ssd_d, ssd_norm=ssd_norm, ssd_w_out=ssd_w_out,
             pool_w_in=pool_w_in, pool_w_grp=pool_w_grp, pool_scale=pool_scale, pool_w_out=pool_w_out,
             xa_w_q=xa_w_q, xa_w_o=xa_w_o, ffn_w_gate=ffn_w_gate, ffn_conv_w=ffn_conv_w,
             ffn_conv_b=ffn_conv_b, ffn_w_up=ffn_w_up, ffn_w_down=ffn_w_down)
    sw = _ssd_weights(p)
    depth = norm_mix.shape[0]
    bp, n_p, d = x_prompt.shape
    bs, n_s, _ = x_sample.shape
    n_mem = mem_prompt.shape[1]
    hd = d // XA_HEADS

    mem_k_p, mem_v_p = mem_kv(mem_prompt.reshape(bp * n_mem, d), norm_mem, xa_w_k, xa_w_v)
    mem_k_p = mem_k_p.reshape(depth, bp, n_mem, d)
    mem_v_p = mem_v_p.reshape(depth, bp, n_mem, d)

    y_p, ssm_p, sconv_p, pool_p, fconv_p = _trunk(
        x_prompt.reshape(bp * n_p, d), p, sw, prompt=True, bsz=bp, mem_k=mem_k_p, mem_v=mem_v_p)

    xs_tm = x_sample.transpose(1, 0, 2).reshape(n_s * bs, d)
    y_s, ssm_s, sconv_s, pool_s, fconv_s = _trunk(
        xs_tm, p, sw, prompt=False, bsz=bs,
        mem_k=cache_mem_k, mem_v=cache_mem_v,
        state_ssm=state_ssm, state_ssm_conv=state_ssm_conv, state_pool=state_pool,
        state_ffn_conv=state_ffn_conv)
    y_s = y_s.reshape(n_s, bs, d).transpose(1, 0, 2)

    kv_shape = (depth, bp, n_mem, XA_HEADS, hd)
    return (y_p.reshape(bp, n_p, d), y_s, mem_k_p.reshape(kv_shape), mem_v_p.reshape(kv_shape),
            ssm_p, sconv_p, pool_p, fconv_p, ssm_s, sconv_s, pool_s, fconv_s)
```

```python
import functools

import jax
import jax.numpy as jnp
from jax import lax
from jax.experimental import pallas as pl
from jax.experimental.pallas import tpu as pltpu

F32 = jnp.float32
BF16 = jnp.bfloat16

EPS = 1e-6
PAST_LEN = 16384
POOL_WINDOWS = (2, 4, 8, 16)

SSD_HEAD_DIM = 64
SSD_HEADS = 64
SSD_GROUPS = 8
SSD_HPG = SSD_HEADS // SSD_GROUPS
D_STATE = 128
D_INNER = SSD_HEADS * SSD_HEAD_DIM
SSD_GDIM = D_INNER // SSD_GROUPS
SSD_BC = SSD_GROUPS * D_STATE
SSD_CONV_DIM = D_INNER + 2 * SSD_BC
SSD_CONV = 4
SSD_CHUNK = 128
HEADS_PAD = 128

XA_HEADS = 4
FFN_CONV = 3

SUBLANES_V7X = 8
VMEM_LIMIT_V7X = 50 * 1024 * 1024


def _cparams(*sem):
    return pltpu.CompilerParams(dimension_semantics=sem, vmem_limit_bytes=VMEM_LIMIT_V7X)


def _silu(x):
    return x * (0.5 * jnp.tanh(0.5 * x) + 0.5)


def _softplus(x):
    return jnp.maximum(x, 0.0) + jnp.log(1.0 + jnp.exp(-jnp.abs(x)))


def _dot(a, b):
    return jnp.dot(a, b, preferred_element_type=F32)


def _dot_nt(a, b):
    return lax.dot_general(a, b, (((1,), (1,)), ((), ())), preferred_element_type=F32)


def _dot_tn(a, b):
    return lax.dot_general(a, b, (((0,), (0,)), ((), ())), preferred_element_type=F32)


def _split3(x):
    hi = x.astype(BF16)
    r1 = x - hi.astype(F32)
    mid = r1.astype(BF16)
    lo = (r1 - mid.astype(F32)).astype(BF16)
    return hi, mid, lo


def _dot_split_rhs01(parts, e01):
    return _dot(parts[0], e01) + _dot(parts[1], e01) + _dot(parts[2], e01)


def _dot_exact_rhs01(x, e01):
    return _dot_split_rhs01(_split3(x), e01)


def _dot_exact_lhs01(e01, x):
    hi, mid, lo = _split3(x)
    return _dot(e01, hi) + _dot(e01, mid) + _dot(e01, lo)


def _head_expand_matrix(first_head, width):
    row = lax.broadcasted_iota(jnp.int32, (HEADS_PAD, width), 0)
    lane = lax.broadcasted_iota(jnp.int32, (HEADS_PAD, width), 1)
    return jnp.where(row == first_head + lane // SSD_HEAD_DIM, 1.0, 0.0).astype(BF16)


def _rms_kernel(x_ref, g_ref, o_ref):
    x = x_ref[...]
    ms = jnp.mean(x * x, axis=-1, keepdims=True)
    o_ref[...] = (x * lax.rsqrt(ms + EPS) * g_ref[...]).astype(o_ref.dtype)


def rmsnorm(x, g_stack, layer, out_dtype):
    m, d = x.shape
    tm = min(m, 1024)
    g3 = g_stack.reshape(-1, 1, d)
    return pl.pallas_call(
        _rms_kernel,
        grid=(m // tm,),
        in_specs=[pl.BlockSpec((tm, d), lambda i: (i, 0)),
                  pl.BlockSpec((None, 1, d), lambda i: (layer, 0, 0))],
        out_specs=pl.BlockSpec((tm, d), lambda i: (i, 0)),
        out_shape=jax.ShapeDtypeStruct((m, d), out_dtype),
        compiler_params=_cparams("parallel"),
        name="rmsnorm",
    )(x, g3)


def _mm_kernel(x_ref, w_ref, *rest, has_res, w_is_nk):
    o_ref = rest[-1]
    w = w_ref[...].astype(BF16)
    acc = _dot_nt(x_ref[...], w) if w_is_nk else _dot(x_ref[...], w)
    if has_res:
        acc = acc + rest[0][...]
    o_ref[...] = acc.astype(o_ref.dtype)


def matmul(x, w_stack, layer, *, ncols=None, res=None, out_dtype=F32, tn=512, w_is_nk=False):
    m, k = x.shape
    n = ncols or w_stack.shape[1 if w_is_nk else 2]
    tm = 2048 if (k <= 2048 and m % 2048 == 0) else min(m, 1024)
    tn = min(tn, n)
    assert m % tm == 0 and n % tn == 0
    w_spec = (pl.BlockSpec((None, tn, k), lambda i, j: (layer, j, 0)) if w_is_nk
              else pl.BlockSpec((None, k, tn), lambda i, j: (layer, 0, j)))
    in_specs = [pl.BlockSpec((tm, k), lambda i, j: (i, 0)), w_spec]
    args = [x, w_stack]
    if res is not None:
        in_specs.append(pl.BlockSpec((tm, tn), lambda i, j: (i, j)))
        args.append(res)
    return pl.pallas_call(
        functools.partial(_mm_kernel, has_res=res is not None, w_is_nk=w_is_nk),
        grid=(m // tm, n // tn),
        in_specs=in_specs,
        out_specs=pl.BlockSpec((tm, tn), lambda i, j: (i, j)),
        out_shape=jax.ShapeDtypeStruct((m, n), out_dtype),
        compiler_params=_cparams("parallel", "arbitrary"),
        name="matmul",
    )(*args)


def _norm_mm_kernel(x_ref, g_ref, w_ref, o_ref, xn_ref):
    @pl.when(pl.program_id(1) == 0)
    def _():
        _rms_kernel(x_ref, g_ref, xn_ref)

    o_ref[...] = _dot(xn_ref[...], w_ref[...].astype(BF16)).astype(o_ref.dtype)


def norm_matmul(x, g_stack, w_stack, layer, *, out_dtype, tn=512):
    m, k = x.shape
    n = w_stack.shape[2]
    tm = min(m, 1024)
    return pl.pallas_call(
        _norm_mm_kernel,
        grid=(m // tm, n // tn),
        in_specs=[pl.BlockSpec((tm, k), lambda i, j: (i, 0)),
                  pl.BlockSpec((None, 1, k), lambda i, j: (layer, 0, 0)),
                  pl.BlockSpec((None, k, tn), lambda i, j: (layer, 0, j))],
        out_specs=pl.BlockSpec((tm, tn), lambda i, j: (i, j)),
        out_shape=jax.ShapeDtypeStruct((m, n), out_dtype),
        scratch_shapes=[pltpu.VMEM((tm, k), BF16)],
        compiler_params=_cparams("parallel", "arbitrary"),
        name="norm_matmul",
    )(x, g_stack.reshape(-1, 1, k), w_stack)


def _mem_kv_kernel(x_ref, g_ref, wk_ref, wv_ref, k_ref, v_ref, xn_ref):
    @pl.when(pl.program_id(1) == 0)
    def _():
        x = x_ref[...]
        ms = jnp.mean(x * x, axis=-1, keepdims=True)
        xn_ref[...] = (x * lax.rsqrt(ms + EPS) * g_ref[...]).astype(BF16)

    xn = xn_ref[...]
    k_ref[...] = _dot(xn, wk_ref[...].astype(BF16))
    v_ref[...] = _dot(xn, wv_ref[...].astype(BF16))


def mem_kv(mem, g_stack, w_k, w_v, tn=256):
    m, d = mem.shape
    nl = w_k.shape[0]
    w_spec = pl.BlockSpec((None, d, tn), lambda l, j: (l, 0, j))
    o_spec = pl.BlockSpec((None, m, tn), lambda l, j: (l, 0, j))
    out = jax.ShapeDtypeStruct((nl, m, d), F32)
    return pl.pallas_call(
        _mem_kv_kernel,
        grid=(nl, d // tn),
        in_specs=[pl.BlockSpec((m, d), lambda l, j: (0, 0)),
                  pl.BlockSpec((None, 1, d), lambda l, j: (l, 0, 0)), w_spec, w_spec],
        out_specs=[o_spec, o_spec],
        out_shape=[out, out],
        scratch_shapes=[pltpu.VMEM((m, d), BF16)],
        compiler_params=_cparams("arbitrary", "arbitrary"),
        name="mem_kv",
    )(mem, g_stack.reshape(nl, 1, d), w_k, w_v)


def _ssd_chunk_kernel(z0_ref, z1_ref, x0_ref, x1_ref, bc_ref, dtr_ref, cw_ref, cb_ref,
                      dtb_ref, alog_ref, dskip_ref, gn_ref,
                      y_ref, h_ref, tail_ref, ext_ref, xbc_ref, yg_ref):
    q = x0_ref.shape[0]
    hist = SUBLANES_V7X
    c = pl.program_id(1)

    @pl.when(c == 0)
    def _():
        ext_ref[0:hist, :] = jnp.zeros((hist, SSD_CONV_DIM), F32)
        h_ref[...] = jnp.zeros(h_ref.shape, F32)

    half = D_INNER // 2
    ext_ref[hist:hist + q, 0:half] = x0_ref[...]
    ext_ref[hist:hist + q, half:D_INNER] = x1_ref[...]
    ext_ref[hist:hist + q, D_INNER:SSD_CONV_DIM] = bc_ref[...]

    for s in range(SSD_CONV_DIM // SSD_GDIM):
        cols = slice(s * SSD_GDIM, (s + 1) * SSD_GDIM)
        full = ext_ref[:, cols]
        acc = cb_ref[:, cols] + full[hist:hist + q] * cw_ref[SSD_CONV - 1:SSD_CONV, cols]
        for k in range(SSD_CONV - 1):
            shifted = pltpu.roll(full, SSD_CONV - 1 - k, axis=0)
            acc = acc + shifted[hist:hist + q] * cw_ref[k:k + 1, cols]
        xbc_ref[:, cols] = _silu(acc)
    ext_ref[0:hist, :] = ext_ref[q:q + hist, :]
    tail_ref[...] = ext_ref[0:hist, :]

    dt = _softplus(dtr_ref[...] + dtb_ref[...])
    da = dt * (-jnp.exp(alog_ref[...]))
    ii = lax.broadcasted_iota(jnp.int32, (q, q), 0)
    jj = lax.broadcasted_iota(jnp.int32, (q, q), 1)
    causal = ii >= jj
    tri = jnp.where(causal, 1.0, 0.0).astype(BF16)
    acum = _dot_exact_lhs01(tri, da)
    acum_t = acum.T
    a_last = acum[q - 1:q, :]
    stack = _split3(jnp.concatenate([dt, jnp.exp(acum), jnp.exp(a_last - acum) * dt], axis=0))
    lane = lax.broadcasted_iota(jnp.int32, (q, 2 * SSD_HEAD_DIM), 1)
    first_of_pair = lane < SSD_HEAD_DIM
    expand = _head_expand_matrix(0, D_INNER)

    for g in range(SSD_GROUPS):
        gcols = slice(g * SSD_GDIM, (g + 1) * SSD_GDIM)
        ex = _dot_split_rhs01(stack, expand[:, gcols])
        dt_e, a_e, wl_e = ex[0:q], ex[q:2 * q], ex[2 * q:3 * q]
        xs = xbc_ref[:, gcols]
        bm = xbc_ref[:, D_INNER + g * D_STATE:D_INNER + (g + 1) * D_STATE].astype(BF16)
        cm = xbc_ref[:, D_INNER + SSD_BC + g * D_STATE:D_INNER + SSD_BC + (g + 1) * D_STATE].astype(BF16)
        xdt = (xs * dt_e).astype(BF16)
        xw = (xs * wl_e).astype(BF16)
        cbm = _dot_nt(cm, bm)
        hg = h_ref[g * SSD_HPG:(g + 1) * SSD_HPG].reshape(SSD_HPG * SSD_HEAD_DIM, D_STATE)
        y_off = _dot_nt(cm, hg.astype(BF16))
        s_new = _dot_tn(xw, bm)
        for pr in range(SSD_HPG // 2):
            pcols = slice(pr * 2 * SSD_HEAD_DIM, (pr + 1) * 2 * SSD_HEAD_DIM)
            ys = []
            for hh in range(2):
                h = g * SSD_HPG + pr * 2 + hh
                diff = acum[:, h:h + 1] - acum_t[h:h + 1, :]
                decay = jnp.exp(jnp.where(causal, diff, -jnp.inf))
                ys.append(_dot((cbm * decay).astype(BF16), xdt[:, pcols]))
            yg_ref[:, pcols] = jnp.where(first_of_pair, ys[0], ys[1]) + a_e[:, pcols] * y_off[:, pcols]
        for r in range(SSD_HPG):
            h = g * SSD_HPG + r
            rows = slice(r * SSD_HEAD_DIM, (r + 1) * SSD_HEAD_DIM)
            h_ref[h] = hg[rows] * jnp.exp(acum_t[h:h + 1, q - 1:q]) + s_new[rows]
        z_ref = z0_ref if g < SSD_GROUPS // 2 else z1_ref
        zc = slice((g % (SSD_GROUPS // 2)) * SSD_GDIM, (g % (SSD_GROUPS // 2) + 1) * SSD_GDIM)
        yv = (yg_ref[...] + xs * dskip_ref[:, gcols]) * _silu(z_ref[:, zc])
        ms = jnp.mean(yv * yv, axis=-1, keepdims=True)
        y_ref[:, gcols] = (yv * lax.rsqrt(ms + EPS) * gn_ref[:, gcols]).astype(y_ref.dtype)


def ssd_prompt(proj, dt_raw, bsz, layer, w):
    m = proj.shape[0]
    n = m // bsz
    q = SSD_CHUNK
    nc = n // q
    half = D_INNER // 2
    row = lambda b, c: b * nc + c
    pspec = lambda cb: pl.BlockSpec((q, half), lambda b, c: (row(b, c), cb))
    vec = lambda width: pl.BlockSpec((None, 1, width), lambda b, c: (layer, 0, 0))
    y, h, tail = pl.pallas_call(
        _ssd_chunk_kernel,
        grid=(bsz, nc),
        in_specs=[pspec(0), pspec(1), pspec(2), pspec(3), pspec(4),
                  pl.BlockSpec((q, HEADS_PAD), lambda b, c: (row(b, c), 0)),
                  pl.BlockSpec((None, SSD_CONV, SSD_CONV_DIM), lambda b, c: (layer, 0, 0)),
                  vec(SSD_CONV_DIM), vec(HEADS_PAD), vec(HEADS_PAD),
                  vec(D_INNER), vec(D_INNER)],
        out_specs=[pl.BlockSpec((q, D_INNER), lambda b, c: (row(b, c), 0)),
                   pl.BlockSpec((None, SSD_HEADS, SSD_HEAD_DIM, D_STATE), lambda b, c: (b, 0, 0, 0)),
                   pl.BlockSpec((None, SUBLANES_V7X, SSD_CONV_DIM), lambda b, c: (b, 0, 0))],
        out_shape=[jax.ShapeDtypeStruct((m, D_INNER), BF16),
                   jax.ShapeDtypeStruct((bsz, SSD_HEADS, SSD_HEAD_DIM, D_STATE), F32),
                   jax.ShapeDtypeStruct((bsz, SUBLANES_V7X, SSD_CONV_DIM), F32)],
        scratch_shapes=[pltpu.VMEM((q + 2 * SUBLANES_V7X, SSD_CONV_DIM), F32),
                        pltpu.VMEM((q, SSD_CONV_DIM), F32),
                        pltpu.VMEM((q, SSD_GDIM), F32)],
        compiler_params=_cparams("parallel", "arbitrary"),
        name="ssd_chunk",
    )(proj, proj, proj, proj, proj, dt_raw, w["conv_w"], w["conv_b"],
      w["dt_bias"], w["a_log"], w["d_skip"], w["g_norm"])
    return y, h, tail[:, SUBLANES_V7X - (SSD_CONV - 1):]


def _conv_time_major(st_ref, x_ref, w_ref, b_ref, nb):
    width = w_ref.shape[0]
    hist = (width - 1) * nb
    rows = x_ref.shape[0]
    acc = b_ref[...]
    for k in range(width):
        parts = []
        if k * nb < hist:
            parts.append(st_ref[k * nb:hist, :])
        take = rows - (hist - k * nb)
        if take > 0:
            parts.append(x_ref[0:take, :])
        piece = parts[0] if len(parts) == 1 else jnp.concatenate(parts, axis=0)
        acc = acc + piece[0:rows] * w_ref[k:k + 1, :]
    return _silu(acc)


def _ssd_sample_prep_kernel(xr_ref, br_ref, cr_ref, z_ref, sx_ref, sb_ref, sc_ref, wx_ref, wb_ref, wc_ref,
                            bx_ref, bb_ref, bcb_ref, dtr_ref, dtb_ref, alog_ref, dskip_ref,
                            y2_ref, ae_ref, xw_ref, bm_ref, cm_ref, zo_ref, dl_ref, *, nb, nt):
    g = pl.program_id(0)
    xs = _conv_time_major(sx_ref, xr_ref, wx_ref, bx_ref, nb)
    bm = _conv_time_major(sb_ref, br_ref, wb_ref, bb_ref, nb)
    cm = _conv_time_major(sc_ref, cr_ref, wc_ref, bcb_ref, nb)
    bmr = bm.astype(BF16).astype(F32)
    cmr = cm.astype(BF16).astype(F32)

    dt = _softplus(dtr_ref[...] + dtb_ref[...])
    da = dt * (-jnp.exp(alog_ref[...]))
    sl = lambda a, t: a[t * nb:(t + 1) * nb]
    acums = [sl(da, 0)]
    for t in range(1, nt):
        acums.append(acums[-1] + sl(da, t))
    dl_ref[...] = jnp.exp(acums[-1])

    row = lax.broadcasted_iota(jnp.int32, (HEADS_PAD, SSD_GDIM), 0)
    lane = lax.broadcasted_iota(jnp.int32, (HEADS_PAD, SSD_GDIM), 1)
    e01 = jnp.where(row == g * SSD_HPG + lane // SSD_HEAD_DIM, 1.0, 0.0).astype(BF16)
    ex = _dot_exact_rhs01(jnp.concatenate(acums + [dt], axis=0), e01)
    a_e = [sl(ex, t) for t in range(nt)]
    d_e = [sl(ex, nt + t) for t in range(nt)]
    for t in range(nt):
        y2 = sl(xs, t) * dskip_ref[...]
        for j in range(t + 1):
            cb = jnp.sum(sl(cmr, t) * sl(bmr, j), axis=-1, keepdims=True)
            y2 = y2 + jnp.exp(a_e[t] - a_e[j]) * d_e[j] * cb * sl(xs, j)
        y2_ref[:, t, :] = y2
        ae_ref[:, t, :] = jnp.exp(a_e[t])
        xw_ref[:, t, :] = jnp.exp(a_e[nt - 1] - a_e[t]) * d_e[t] * sl(xs, t)
        bm_ref[:, t, :] = sl(bm, t)
        cm_ref[:, t, :] = sl(cm, t)
        zo_ref[:, t, :] = z_ref[t * nb:(t + 1) * nb, :]


def _ssd_sample_state_kernel(dl_ref, h0_ref, cm_ref, bm_ref, xw_ref, ae_ref, y2_ref, z_ref, gn_ref, *rest,
                             n_prev):
    if n_prev:
        prev_ref, y_ref, h_ref = rest
        for l in range(n_prev):
            h_ref[l] = prev_ref[l]
    else:
        y_ref, h_ref = rest
    nbb = h0_ref.shape[0]
    for bb in range(nbb):
        b = pl.program_id(0) * nbb + bb
        for g in range(SSD_GROUPS):
            gcols = slice(g * SSD_GDIM, (g + 1) * SSD_GDIM)
            scols = slice(g * D_STATE, (g + 1) * D_STATE)
            hg = h0_ref[bb, g * SSD_HPG:(g + 1) * SSD_HPG].reshape(SSD_HPG * SSD_HEAD_DIM, D_STATE)
            y_off = _dot_nt(cm_ref[bb, :, scols].astype(BF16), hg.astype(BF16))
            s_new = _dot_tn(xw_ref[bb, :, gcols].astype(BF16), bm_ref[bb, :, scols].astype(BF16))
            for r in range(SSD_HPG):
                h = g * SSD_HPG + r
                rows = slice(r * SSD_HEAD_DIM, (r + 1) * SSD_HEAD_DIM)
                h_ref[n_prev, bb, h] = hg[rows] * dl_ref[b, h] + s_new[rows]
            yv = (ae_ref[bb, :, gcols] * y_off + y2_ref[bb, :, gcols]) * _silu(z_ref[bb, :, gcols])
            ms = jnp.mean(yv * yv, axis=-1, keepdims=True)
            y_ref[bb, :, gcols] = yv * lax.rsqrt(ms + EPS) * gn_ref[:, gcols]


def ssd_sample(proj, dt_raw, conv_state, h0_stack, h_prev, nb, layer, w):
    m = proj.shape[0]
    nt = m // nb
    hist = (SSD_CONV - 1) * nb
    xblk = D_INNER // SSD_GDIM
    bblk = 2 * D_INNER // D_STATE
    cblk = bblk + SSD_GROUPS
    sbb = D_INNER // D_STATE
    scb = sbb + SSD_GROUPS
    vec = lambda width: pl.BlockSpec((None, 1, width), lambda g: (layer, 0, 0))
    bm_out = lambda width: jax.ShapeDtypeStruct((nb, nt, width), F32)
    bm_spec = lambda width: pl.BlockSpec((nb, nt, width), lambda g: (0, 0, g))
    y2, a_e, xw, bm, cm, z_bm, dlast = pl.pallas_call(
        functools.partial(_ssd_sample_prep_kernel, nb=nb, nt=nt),
        grid=(SSD_GROUPS,),
        in_specs=[pl.BlockSpec((m, SSD_GDIM), lambda g: (0, xblk + g)),
                  pl.BlockSpec((m, D_STATE), lambda g: (0, bblk + g)),
                  pl.BlockSpec((m, D_STATE), lambda g: (0, cblk + g)),
                  pl.BlockSpec((m, SSD_GDIM), lambda g: (0, g)),
                  pl.BlockSpec((hist, SSD_GDIM), lambda g: (0, g)),
                  pl.BlockSpec((hist, D_STATE), lambda g: (0, sbb + g)),
                  pl.BlockSpec((hist, D_STATE), lambda g: (0, scb + g)),
                  pl.BlockSpec((None, SSD_CONV, SSD_GDIM), lambda g: (layer, 0, g)),
                  pl.BlockSpec((None, SSD_CONV, D_STATE), lambda g: (layer, 0, sbb + g)),
                  pl.BlockSpec((None, SSD_CONV, D_STATE), lambda g: (layer, 0, scb + g)),
                  pl.BlockSpec((None, 1, SSD_GDIM), lambda g: (layer, 0, g)),
                  pl.BlockSpec((None, 1, D_STATE), lambda g: (layer, 0, sbb + g)),
                  pl.BlockSpec((None, 1, D_STATE), lambda g: (layer, 0, scb + g)),
                  pl.BlockSpec((m, HEADS_PAD), lambda g: (0, 0)),
                  vec(HEADS_PAD), vec(HEADS_PAD),
                  pl.BlockSpec((None, 1, SSD_GDIM), lambda g: (layer, 0, g))],
        out_specs=[bm_spec(SSD_GDIM), bm_spec(SSD_GDIM), bm_spec(SSD_GDIM), bm_spec(D_STATE), bm_spec(D_STATE),
                   bm_spec(SSD_GDIM), pl.BlockSpec((nb, HEADS_PAD), lambda g: (0, 0))],
        out_shape=[bm_out(D_INNER), bm_out(D_INNER), bm_out(D_INNER), bm_out(SSD_BC), bm_out(SSD_BC),
                   bm_out(D_INNER), jax.ShapeDtypeStruct((nb, HEADS_PAD), F32)],
        compiler_params=_cparams("arbitrary"),
        name="ssd_sample_prep",
    )(proj, proj, proj, proj, conv_state, conv_state, conv_state, w["conv_w"], w["conv_w"], w["conv_w"],
      w["conv_b"], w["conv_b"], w["conv_b"], dt_raw, w["dt_bias"], w["a_log"], w["d_skip"])

    n_prev = 0 if h_prev is None else h_prev.shape[0]
    state_dims = (SSD_HEADS, SSD_HEAD_DIM, D_STATE)
    nbb = 2 if nb % 2 == 0 else 1
    per_b = lambda width: pl.BlockSpec((nbb, nt, width), lambda b: (b, 0, 0))
    in_specs = [pl.BlockSpec(memory_space=pltpu.SMEM),
                pl.BlockSpec((None, nbb) + state_dims, lambda b: (layer, b, 0, 0, 0)),
                per_b(SSD_BC), per_b(SSD_BC), per_b(D_INNER), per_b(D_INNER), per_b(D_INNER), per_b(D_INNER),
                pl.BlockSpec((None, 1, D_INNER), lambda b: (layer, 0, 0))]
    args = [dlast[:, :SSD_HEADS], h0_stack, cm, bm, xw, a_e, y2, z_bm, w["g_norm"]]
    if n_prev:
        in_specs.append(pl.BlockSpec((n_prev, nbb) + state_dims, lambda b: (0, b, 0, 0, 0)))
        args.append(h_prev)
    y, h = pl.pallas_call(
        functools.partial(_ssd_sample_state_kernel, n_prev=n_prev),
        grid=(nb // nbb,),
        in_specs=in_specs,
        out_specs=[per_b(D_INNER),
                   pl.BlockSpec((n_prev + 1, nbb) + state_dims, lambda b: (0, b, 0, 0, 0))],
        out_shape=[jax.ShapeDtypeStruct((nb, nt, D_INNER), F32),
                   jax.ShapeDtypeStruct((n_prev + 1, nb) + state_dims, F32)],
        compiler_params=_cparams("parallel"),
        name="ssd_sample_state",
    )(*args)
    y_tm = y.transpose(1, 0, 2).reshape(m, D_INNER).astype(BF16)
    return y_tm, h


def _pool_kernel(*refs, tm, stride, halo, tail, tiles_per_seq, pos0, has_state):
    if has_state:
        x_ref, gn_ref, win_ref, wg_ref, sc_ref, st_ref, mixed_ref, vtail_ref, u_ref, ext_ref = refs
    else:
        x_ref, gn_ref, win_ref, wg_ref, sc_ref, mixed_ref, vtail_ref, u_ref, ext_ref, carry_ref = refs
    i = pl.program_id(0)
    g = pl.program_id(1)
    width = ext_ref.shape[1]

    @pl.when(g == 0)
    def _():
        _rms_kernel(x_ref, gn_ref, u_ref)

    if has_state:
        ext_ref[0:halo, :] = st_ref[...]
    else:
        first = i % tiles_per_seq == 0

        @pl.when(first)
        def _():
            ext_ref[0:halo, :] = jnp.zeros((halo, width), F32)

        @pl.when(jnp.logical_not(first))
        def _():
            ext_ref[0:halo, :] = carry_ref[g]

    v = _dot(u_ref[...], win_ref[...].astype(BF16))
    ext_ref[halo:halo + tm, :] = v
    vtail_ref[...] = v[tm - tail:tm]
    if not has_state:
        carry_ref[g] = v[tm - halo:tm]
    t0 = (i % tiles_per_seq) * (tm // stride)
    t = t0 + lax.broadcasted_iota(jnp.int32, (tm, width), 0) // stride

    for gi, win in enumerate(POOL_WINDOWS):
        @pl.when(g == gi)
        def _(win=win):
            s = v
            for d in range(1, win):
                s = s + ext_ref[pl.ds(halo - d * stride, tm), :]
            cnt = jnp.minimum(pos0 + t + 1, win).astype(F32)
            p = (s / cnt - v).astype(BF16)
            mixed_ref[...] = (_dot(p, wg_ref[...].astype(BF16)) * sc_ref[...]).astype(mixed_ref.dtype)


def pool_mix(x, g_stack, g_layer, w_in, w_grp, scale3, layer, *, n_seq, stride, pos0, state=None):
    m, d = x.shape
    ng = len(POOL_WINDOWS)
    gdim = w_in.shape[-1] // ng
    has_state = state is not None
    if has_state:
        tm, halo, tail, tiles_per_seq = m, state.shape[0], m, 1
    else:
        tm, halo, tail = min(m // n_seq, 1024), 2 * SUBLANES_V7X, 2 * SUBLANES_V7X
        tiles_per_seq = (m // n_seq) // tm
    in_specs = [pl.BlockSpec((tm, d), lambda i, g: (i, 0)),
                pl.BlockSpec((None, 1, d), lambda i, g: (g_layer, 0, 0)),
                pl.BlockSpec((None, d, gdim), lambda i, g: (layer, 0, g)),
                pl.BlockSpec((None, None, gdim, gdim), lambda i, g: (layer, g, 0, 0)),
                pl.BlockSpec((None, 1, gdim), lambda i, g: (layer, 0, g))]
    args = [x, g_stack.reshape(-1, 1, d), w_in, w_grp, scale3]
    scratch = [pltpu.VMEM((tm, d), BF16), pltpu.VMEM((halo + tm, gdim), F32)]
    if has_state:
        in_specs.append(pl.BlockSpec((halo, gdim), lambda i, g: (0, g)))
        args.append(state)
    else:
        scratch.append(pltpu.VMEM((ng, halo, gdim), F32))
    return pl.pallas_call(
        functools.partial(_pool_kernel, tm=tm, stride=stride, halo=halo, tail=tail,
                          tiles_per_seq=tiles_per_seq, pos0=pos0, has_state=has_state),
        grid=(m // tm, ng),
        in_specs=in_specs,
        out_specs=[pl.BlockSpec((tm, gdim), lambda i, g: (i, g)),
                   pl.BlockSpec((tail, gdim), lambda i, g: (i, g))],
        out_shape=[jax.ShapeDtypeStruct((m, ng * gdim), BF16),
                   jax.ShapeDtypeStruct((m // tm * tail, ng * gdim), F32)],
        scratch_shapes=scratch,
        compiler_params=_cparams("arbitrary", "arbitrary"),
        name="pool_mix",
    )(*args)


def _attn_kernel(q_ref, k_ref, v_ref, o_ref):
    hd = q_ref.shape[-1] // XA_HEADS
    scale = hd ** -0.5
    for h in range(XA_HEADS):
        cols = slice(h * hd, (h + 1) * hd)
        s = _dot_nt(q_ref[:, cols], k_ref[:, cols].astype(BF16)) * scale
        e = jnp.exp(s - jnp.max(s, axis=-1, keepdims=True))
        p = e * (1.0 / jnp.sum(e, axis=-1, keepdims=True))
        o_ref[:, cols] = _dot(p.astype(BF16), v_ref[:, cols].astype(BF16)).astype(o_ref.dtype)


def attention(q, k, v, kv_layer):
    bsz, n, d = q.shape
    n_mem = k.shape[2]
    tq = min(n, 1024)
    kv_spec = pl.BlockSpec((None, None, n_mem, d), lambda b, i: (kv_layer, b, 0, 0))
    return pl.pallas_call(
        _attn_kernel,
        grid=(bsz, n // tq),
        in_specs=[pl.BlockSpec((None, tq, d), lambda b, i: (b, i, 0)), kv_spec, kv_spec],
        out_specs=pl.BlockSpec((None, tq, d), lambda b, i: (b, i, 0)),
        out_shape=jax.ShapeDtypeStruct((bsz, n, d), BF16),
        compiler_params=_cparams("parallel", "arbitrary"),
        name="mem_attention",
    )(q, k, v)


def _attn_cache_kernel(q_ref, k_ref, v_ref, o_ref, *, tp):
    nbb, n_mem, heads, hd = k_ref.shape
    rows = n_mem * heads
    for bb in range(nbb):
        kf = k_ref[bb].reshape(rows, hd).astype(BF16)
        vf = v_ref[bb].reshape(rows, hd).astype(BF16)
        s_t = _dot_nt(kf, q_ref[bb]) * (hd ** -0.5)
        r = lax.broadcasted_iota(jnp.int32, s_t.shape, 0)
        c = lax.broadcasted_iota(jnp.int32, s_t.shape, 1)
        s_t = jnp.where(r % heads == c // tp, s_t, -jnp.inf)
        e = jnp.exp(s_t - jnp.max(s_t, axis=0, keepdims=True))
        p_t = e * (1.0 / jnp.sum(e, axis=0, keepdims=True))
        o_ref[bb] = _dot_tn(p_t.astype(BF16), vf).astype(o_ref.dtype)


def attention_cache(q, k, v, kv_layer, tp):
    bsz = q.shape[0]
    nbb = 4 if bsz % 4 == 0 else 1
    kv_spec = pl.BlockSpec((None, nbb) + k.shape[2:], lambda b: (kv_layer, b, 0, 0, 0))
    q_spec = pl.BlockSpec((nbb,) + q.shape[1:], lambda b: (b, 0, 0))
    return pl.pallas_call(
        functools.partial(_attn_cache_kernel, tp=tp),
        grid=(bsz // nbb,),
        in_specs=[q_spec, kv_spec, kv_spec],
        out_specs=q_spec,
        out_shape=jax.ShapeDtypeStruct(q.shape, BF16),
        compiler_params=_cparams("parallel"),
        name="cache_attention",
    )(q, k, v)


def _ffn_gate_kernel(*refs, tm, stride, halo, tail, tiles_per_seq, has_state):
    if has_state:
        u_ref, wg_ref, wu_ref, cw_ref, cb_ref, st_ref, h_ref, gtail_ref, ext_ref = refs
    else:
        u_ref, wg_ref, wu_ref, cw_ref, cb_ref, h_ref, gtail_ref, ext_ref, carry_ref = refs
    i = pl.program_id(0)
    j = pl.program_id(1)
    u = u_ref[...]
    if has_state:
        ext_ref[0:halo, :] = st_ref[...]
    else:
        first = i % tiles_per_seq == 0

        @pl.when(first)
        def _():
            ext_ref[0:halo, :] = jnp.zeros((halo, ext_ref.shape[1]), F32)

        @pl.when(jnp.logical_not(first))
        def _():
            ext_ref[0:halo, :] = carry_ref[j]

    gate = _dot(u, wg_ref[...].astype(BF16))
    up = _dot(u, wu_ref[...].astype(BF16))
    ext_ref[halo:halo + tm, :] = gate
    acc = cb_ref[...] + gate * cw_ref[FFN_CONV - 1:FFN_CONV, :]
    for k in range(FFN_CONV - 1):
        acc = acc + ext_ref[pl.ds(halo - (FFN_CONV - 1 - k) * stride, tm), :] * cw_ref[k:k + 1, :]
    h_ref[...] = (_silu(acc) * up).astype(h_ref.dtype)
    gtail_ref[...] = ext_ref[halo + tm - tail:halo + tm, :]
    if not has_state:
        carry_ref[j] = ext_ref[tm:tm + halo, :]


def ffn_gate(u, w_gate, w_up, conv_w, conv_b3, layer, *, n_seq, stride, state=None, tn=512):
    m, d = u.shape
    nf = w_gate.shape[-1]
    has_state = state is not None
    if has_state:
        tm, halo, tail, tiles_per_seq = m, state.shape[0], state.shape[0], 1
    else:
        tm, halo, tail = min(m // n_seq, 1024), SUBLANES_V7X, SUBLANES_V7X
        tiles_per_seq = (m // n_seq) // tm
    wspec = pl.BlockSpec((None, d, tn), lambda i, j: (layer, 0, j))
    in_specs = [pl.BlockSpec((tm, d), lambda i, j: (i, 0)), wspec, wspec,
                pl.BlockSpec((None, FFN_CONV, tn), lambda i, j: (layer, 0, j)),
                pl.BlockSpec((None, 1, tn), lambda i, j: (layer, 0, j))]
    args = [u, w_gate, w_up, conv_w, conv_b3]
    scratch = [pltpu.VMEM((halo + tm, tn), F32)]
    if has_state:
        in_specs.append(pl.BlockSpec((halo, tn), lambda i, j: (0, j)))
        args.append(state)
    else:
        scratch.append(pltpu.VMEM((nf // tn, halo, tn), F32))
    return pl.pallas_call(
        functools.partial(_ffn_gate_kernel, tm=tm, stride=stride, halo=halo, tail=tail,
                          tiles_per_seq=tiles_per_seq, has_state=has_state),
        grid=(m // tm, nf // tn),
        in_specs=in_specs,
        out_specs=[pl.BlockSpec((tm, tn), lambda i, j: (i, j)),
                   pl.BlockSpec((tail, tn), lambda i, j: (i, j))],
        out_shape=[jax.ShapeDtypeStruct((m, nf), BF16),
                   jax.ShapeDtypeStruct((m // tm * tail, nf), F32)],
        scratch_shapes=scratch,
        compiler_params=_cparams("arbitrary", "arbitrary"),
        name="ffn_gate",
    )(*args)


def _ssd_weights(p):
    pad = HEADS_PAD - SSD_HEADS
    nl = p["ssd_dt_bias"].shape[0]
    padv = lambda a: jnp.pad(a, ((0, 0), (0, pad)))
    dt_bias = padv(p["ssd_dt_bias"])
    a_log = padv(p["ssd_a_log"])
    return dict(
        w_in_nk=jnp.swapaxes(p["ssd_w_in"], 1, 2),
        w_dt=jnp.pad(p["ssd_w_in"][:, :, D_INNER + SSD_CONV_DIM:], ((0, 0), (0, 0), (0, pad))),
        conv_w=p["ssd_conv_w"],
        conv_b=p["ssd_conv_b"].reshape(nl, 1, SSD_CONV_DIM),
        dt_bias=dt_bias.reshape(nl, 1, HEADS_PAD),
        a_log=a_log.reshape(nl, 1, HEADS_PAD),
        d_skip=jnp.repeat(p["ssd_d"], SSD_HEAD_DIM, axis=1).reshape(nl, 1, D_INNER),
        g_norm=p["ssd_norm"].reshape(nl, 1, D_INNER))


def _trunk(x, p, sw, *, prompt, bsz, mem_k, mem_v, state_ssm=None, state_ssm_conv=None, state_pool=None,
           state_ffn_conv=None):
    m, d = x.shape
    n = m // bsz
    depth = p["norm_mix"].shape[0]
    stride = 1 if prompt else bsz
    n_seq = bsz if prompt else 1
    proj_cols = D_INNER + SSD_CONV_DIM
    ssm_out, sconv_out, pool_out, fconv_out = [], [], [], []
    ssm_all = None

    def time_major_state(s):
        return s.transpose(1, 0, 2).reshape(s.shape[1] * s.shape[0], s.shape[2])

    for i in range(depth):
        j = i // 2
        if i % 2 == 0:
            u = rmsnorm(x, p["norm_mix"], i, BF16)
            proj = matmul(u, sw["w_in_nk"], j, ncols=proj_cols, w_is_nk=True)
            dt_raw = matmul(u, sw["w_dt"], j)
            if prompt:
                y, h, sconv_tail = ssd_prompt(proj, dt_raw, bsz, j, sw)
                ssm_out.append(h)
                sconv_out.append(sconv_tail)
            else:
                y, ssm_all = ssd_sample(proj, dt_raw, time_major_state(state_ssm_conv[j]), state_ssm, ssm_all,
                                        bsz, j, sw)
                xbc = proj.reshape(n, bsz, proj_cols)[:, :, D_INNER:]
                ext = jnp.concatenate([state_ssm_conv[j], xbc.transpose(1, 0, 2)], axis=1)
                sconv_out.append(ext[:, -(SSD_CONV - 1):])
            x = matmul(y, p["ssd_w_out"], j, res=x)
        else:
            scale3 = p["pool_scale"].reshape(-1, 1, p["pool_scale"].shape[-1])
            if prompt:
                mixed, vtail = pool_mix(x, p["norm_mix"], i, p["pool_w_in"], p["pool_w_grp"], scale3, j,
                                        n_seq=n_seq, stride=1, pos0=0)
                nbuf = max(POOL_WINDOWS) - 1
                pool_out.append(vtail.reshape(bsz, -1, vtail.shape[-1])[:, -nbuf:])
            else:
                mixed, vtail = pool_mix(x, p["norm_mix"], i, p["pool_w_in"], p["pool_w_grp"], scale3, j,
                                        n_seq=n_seq, stride=bsz, pos0=PAST_LEN,
                                        state=time_major_state(state_pool[j]))
                v_bm = vtail.reshape(n, bsz, -1).transpose(1, 0, 2)
                nbuf = state_pool.shape[2]
                pool_out.append(jnp.concatenate([state_pool[j], v_bm], axis=1)[:, -nbuf:])
            x = matmul(mixed, p["pool_w_out"], j, res=x)

        qm = norm_matmul(x, p["norm_xattn"], p["xa_w_q"], i, out_dtype=BF16)
        if prompt:
            o = attention(qm.reshape(bsz, n, d), mem_k, mem_v, i).reshape(m, d)
        else:
            hd = d // XA_HEADS
            q4 = qm.reshape(n, bsz, XA_HEADS, hd).transpose(1, 2, 0, 3).reshape(bsz, XA_HEADS * n, hd)
            o4 = attention_cache(q4, mem_k, mem_v, i, n).reshape(bsz, XA_HEADS, n, hd)
            o = o4.transpose(2, 0, 1, 3).reshape(m, d)
        x = matmul(o, p["xa_w_o"], i, res=x)

        u = rmsnorm(x, p["norm_ffn"], i, BF16)
        cb3 = p["ffn_conv_b"].reshape(depth, 1, -1)
        if prompt:
            hcat, gtail = ffn_gate(u, p["ffn_w_gate"], p["ffn_w_up"], p["ffn_conv_w"], cb3, i, n_seq=n_seq,
                                   stride=1)
            fconv_out.append(gtail.reshape(bsz, -1, gtail.shape[-1])[:, -(FFN_CONV - 1):])
        else:
            hcat, gtail = ffn_gate(u, p["ffn_w_gate"], p["ffn_w_up"], p["ffn_conv_w"], cb3, i, n_seq=n_seq,
                                   stride=bsz, state=time_major_state(state_ffn_conv[i]))
            fconv_out.append(gtail.reshape(FFN_CONV - 1, bsz, -1).transpose(1, 0, 2))
        x = matmul(hcat, p["ffn_w_down"], i, res=x)

    y = rmsnorm(x, p["norm_final"].reshape(1, d), 0, F32)
    ssm_new = jnp.stack(ssm_out) if prompt else ssm_all
    return y, ssm_new, jnp.stack(sconv_out), jnp.stack(pool_out), jnp.stack(fconv_out)


def kernel(x_prompt, x_sample, mem_prompt, cache_mem_k, cache_mem_v, state_ssm, state_ssm_conv, state_pool, state_ffn_conv, norm_mix, norm_xattn, norm_ffn, norm_mem, norm_final, ssd_w_in, ssd_conv_w, ssd_conv_b, ssd_dt_bias, ssd_a_log, ssd_d, ssd_norm, ssd_w_out, pool_w_in, pool_w_grp, pool_scale, pool_w_out, xa_w_q, xa_w_k, xa_w_v, xa_w_o, ffn_w_gate, ffn_conv_w, ffn_conv_b, ffn_w_up, ffn_w_down):
    p = dict(norm_mix=norm_mix, norm_xattn=norm_xattn, norm_ffn=norm_ffn, norm_final=norm_final,
             ssd_w_in=ssd_w_in, ssd_conv_w=ssd_conv_w, ssd_conv_b=ssd_conv_b, ssd_dt_bias=ssd_dt_bias,
             ssd_a_log=ssd_a_log, ssd_d=ssd_d, ssd_norm=ssd_norm, ssd_w_out=ssd_w_out,
             pool_w_in=pool_w_in, pool_w_grp=pool_w_grp, pool_scale=pool_scale, pool_w_out=pool_w_out,
             xa_w_q=xa_w_q, xa_w_o=xa_w_o, ffn_w_gate=ffn_w_gate, ffn_conv_w=ffn_conv_w,
             ffn_conv_b=ffn_conv_b, ffn_w_up=ffn_w_up,
             ffn_w_down=ffn_w_down.astype(BF16))
    sw = _ssd_weights(p)
    depth = norm_mix.shape[0]
    bp, n_p, d = x_prompt.shape
    bs, n_s, _ = x_sample.shape
    n_mem = mem_prompt.shape[1]
    hd = d // XA_HEADS

    mem_k_p, mem_v_p = mem_kv(mem_prompt.reshape(bp * n_mem, d), norm_mem, xa_w_k, xa_w_v)
    mem_k_p = mem_k_p.reshape(depth, bp, n_mem, d)
    mem_v_p = mem_v_p.reshape(depth, bp, n_mem, d)

    y_p, ssm_p, sconv_p, pool_p, fconv_p = _trunk(
        x_prompt.reshape(bp * n_p, d), p, sw, prompt=True, bsz=bp, mem_k=mem_k_p, mem_v=mem_v_p)

    xs_tm = x_sample.transpose(1, 0, 2).reshape(n_s * bs, d)
    y_s, ssm_s, sconv_s, pool_s, fconv_s = _trunk(
        xs_tm, p, sw, prompt=False, bsz=bs,
        mem_k=cache_mem_k, mem_v=cache_mem_v,
        state_ssm=state_ssm, state_ssm_conv=state_ssm_conv, state_pool=state_pool,
        state_ffn_conv=state_ffn_conv)
    y_s = y_s.reshape(n_s, bs, d).transpose(1, 0, 2)

    kv_shape = (depth, bp, n_mem, XA_HEADS, hd)
    return (y_p.reshape(bp, n_p, d), y_s, mem_k_p.reshape(kv_shape), mem_v_p.reshape(kv_shape),
            ssm_p, sconv_p, pool_p, fconv_p, ssm_s, sconv_s, pool_s, fconv_s)
```
